```python
import jax, jax.numpy as jnp
from jax import lax
import numpy as np

D_MODEL = 2048
BATCH = 32
SEQ = 256
DEPTH = 1
DEC_BATCH = 4
DEC_SEQ = 4096
PAST_LEN = 256

GRID_W = 64
H_A = 8
Q_LORA = 512
KV_LORA = 256
NOPE_DIM = 128
ROPE_DIM = 64
V_DIM = 128
H_B = 8
HD_B = 128
NA_W = H_B * HD_B
WIN_R_MAX = 8
WIN_C = 16
COL_QBLOCK = 16
COL_KBLOCK = COL_QBLOCK + WIN_C
D_FF = -(-8 * D_MODEL // (3 * 256)) * 256
IN_COLS = Q_LORA + KV_LORA + ROPE_DIM + 3 * NA_W + 2 * D_MODEL
ROPE_THETA = 10000.0
NORM_EPS = 1e-6
Q_BLOCK = 128
NEG_INF = -1e30
MLA_SCALE = (NOPE_DIM + ROPE_DIM) ** -0.5
NA_SCALE = HD_B ** -0.5

kernel_name = 'hybrid_mla_natten_prefix_dit_step'


def rmsnorm(x, g):
    xf = x.astype(jnp.float32)
    xf = xf * lax.rsqrt(jnp.mean(xf * xf, axis=-1, keepdims=True) + NORM_EPS)
    return (xf * g.astype(jnp.float32)).astype(x.dtype)


def modulate(x, shift, scale):
    return x * (1 + scale) + shift


def adaln_params(cond, w_mod, b_mod):
    m = jax.nn.silu(cond) @ w_mod + b_mod
    return jnp.split(m, 6, axis=-1)


def axial_rope_tables(n_tokens):
    t = jnp.arange(n_tokens, dtype=jnp.int32)
    row = (t // GRID_W).astype(jnp.float32)
    col = (t % GRID_W).astype(jnp.float32)
    n_freq = ROPE_DIM // 4
    inv_freq = ROPE_THETA ** (-jnp.arange(n_freq, dtype=jnp.float32) / n_freq)
    ang = jnp.concatenate([row[:, None] * inv_freq, col[:, None] * inv_freq], axis=-1)
    return jnp.cos(ang), jnp.sin(ang)


def apply_rope(x, cos, sin):
    half = ROPE_DIM // 2
    xf = x.astype(jnp.float32)
    x1, x2 = xf[..., :half], xf[..., half:]
    return jnp.concatenate([x1 * cos - x2 * sin, x2 * cos + x1 * sin], axis=-1).astype(x.dtype)


def split_projection(p):
    sizes = [Q_LORA, KV_LORA, ROPE_DIM, NA_W, NA_W, NA_W, D_MODEL, D_MODEL]
    return jnp.split(p, [int(i) for i in np.cumsum(sizes)[:-1]], axis=-1)


def split_heads(x, n_heads):
    return x.reshape(x.shape[:-1] + (n_heads, x.shape[-1] // n_heads))


def mla_queries(c_q, q_norm_g, w_uq):
    return split_heads(rmsnorm(c_q, q_norm_g) @ w_uq, H_A)


def mla_keys_values(c_kv, k_rope, w_uk, w_uv):
    b, s, _ = c_kv.shape
    k_nope = (c_kv @ w_uk).reshape(b, s, H_A, NOPE_DIM)
    v = (c_kv @ w_uv).reshape(b, s, H_A, V_DIM)
    k = jnp.concatenate([k_nope, jnp.broadcast_to(k_rope[:, :, None, :], (b, s, H_A, ROPE_DIM))], axis=-1)
    return k, v


def dense_attention(q, k, v, scale):
    b, sq, h, dk = q.shape
    n_blk = sq // Q_BLOCK
    qb = jnp.moveaxis((q * scale).reshape(b, n_blk, Q_BLOCK, h, dk), 1, 0)

    def block(q_blk):
        logits = jnp.einsum('bqhd,bkhd->bhqk', q_blk, k).astype(jnp.float32)
        p = jax.nn.softmax(logits, axis=-1).astype(v.dtype)
        return jnp.einsum('bhqk,bkhd->bqhd', p, v)

    out = lax.map(block, qb)
    return jnp.moveaxis(out, 0, 1).reshape(b, sq, h, v.shape[-1])


def natten_col_tables():
    n_cb = GRID_W // COL_QBLOCK
    kb = np.clip(np.arange(n_cb) * COL_QBLOCK - WIN_C // 2, 0, GRID_W - COL_KBLOCK)
    col_idx = kb[:, None] + np.arange(COL_KBLOCK)[None, :]
    q_col = np.arange(GRID_W).reshape(n_cb, COL_QBLOCK)
    q_start = np.clip(q_col - WIN_C // 2, 0, GRID_W - WIN_C)
    k_col = col_idx[:, None, :]
    in_win = (k_col >= q_start[..., None]) & (k_col < q_start[..., None] + WIN_C)
    col_off = np.clip(k_col - q_col[..., None] + WIN_C - 1, 0, 2 * WIN_C - 2)
    return col_idx, in_win, col_off


def neighborhood_attention(q, k, v, k_ctx, v_ctx, rpb, rows):
    b = q.shape[0]
    win_r = min(WIN_R_MAX, rows)
    n_cb = GRID_W // COL_QBLOCK
    n_win = win_r * COL_KBLOCK
    col_idx, in_win, col_off = natten_col_tables()
    qg = (q * NA_SCALE).reshape(b, rows, GRID_W, H_B, HD_B)
    kg = k.reshape(b, rows, GRID_W, H_B, HD_B)
    vg = v.reshape(b, rows, GRID_W, H_B, HD_B)
    rpb_cols = rpb[:, :, col_off]
    mask = jnp.asarray(in_win)[:, :, None, :]

    def row_step(r):
        q_r = lax.dynamic_index_in_dim(qg, r, axis=1, keepdims=False).reshape(b, n_cb, COL_QBLOCK, H_B, HD_B)
        r0 = jnp.clip(r - win_r // 2, 0, rows - win_r)
        k_blk = lax.dynamic_slice_in_dim(kg, r0, win_r, axis=1)[:, :, col_idx]
        v_blk = lax.dynamic_slice_in_dim(vg, r0, win_r, axis=1)[:, :, col_idx]
        row_off = r0 + jnp.arange(win_r, dtype=jnp.int32) - r + (WIN_R_MAX - 1)
        bias = jnp.transpose(jnp.take(rpb_cols, row_off, axis=1), (0, 2, 3, 1, 4))
        s_win = jnp.einsum('bnqhd,brnkhd->bhnqrk', q_r, k_blk).astype(jnp.float32) + bias.astype(jnp.float32)
        s_win = jnp.where(mask, s_win, NEG_INF).reshape(b, H_B, n_cb, COL_QBLOCK, n_win)
        s_ctx = jnp.einsum('bnqhd,bchd->bhnqc', q_r, k_ctx).astype(jnp.float32)
        p = jax.nn.softmax(jnp.concatenate([s_win, s_ctx], axis=-1), axis=-1).astype(v.dtype)
        p_win = p[..., :n_win].reshape(b, H_B, n_cb, COL_QBLOCK, win_r, COL_KBLOCK)
        o = (jnp.einsum('bhnqrk,brnkhd->bnqhd', p_win, v_blk)
             + jnp.einsum('bhnqc,bchd->bnqhd', p[..., n_win:], v_ctx))
        return o.reshape(b, GRID_W, H_B, HD_B)

    out = lax.map(row_step, jnp.arange(rows, dtype=jnp.int32))
    return jnp.moveaxis(out, 0, 1).reshape(b, rows * GRID_W, H_B, HD_B)


def merge_branches(o_a, o_b, g_a, g_b, w_oa, w_ob, w_out):
    y_a = o_a.reshape(o_a.shape[:2] + (-1,)) @ w_oa
    y_b = o_b.reshape(o_b.shape[:2] + (-1,)) @ w_ob
    return (jax.nn.sigmoid(g_a) * y_a + jax.nn.sigmoid(g_b) * y_b) @ w_out


def swiglu(h, w_gu, w_down):
    gate, up = jnp.split(h @ w_gu, 2, axis=-1)
    return (jax.nn.silu(gate) * up) @ w_down


def setup_inputs(seed: int = 0) -> dict:
    key = jax.random.key(seed)
    ks = jax.random.split(key, 32)
    f32 = jnp.float32

    def nrm(k, shape, s=1.0):
        return s * jax.random.normal(k, shape, f32)

    def gain(k, shape):
        return 1.0 + 0.01 * jax.random.normal(k, shape, f32)

    return {
        'x_prompt': nrm(ks[0], (BATCH, SEQ, D_MODEL)),
        'x_sample': nrm(ks[1], (DEC_BATCH, DEC_SEQ, D_MODEL)),
        'cache_mla_ckv': nrm(ks[2], (DEC_BATCH, DEPTH, PAST_LEN, KV_LORA)),
        'cache_mla_krope': nrm(ks[3], (DEC_BATCH, DEPTH, PAST_LEN, ROPE_DIM)),
        'cache_na_k': nrm(ks[4], (DEC_BATCH, DEPTH, PAST_LEN, H_B, HD_B)),
        'cache_na_v': nrm(ks[5], (DEC_BATCH, DEPTH, PAST_LEN, H_B, HD_B)),
        'c': nrm(ks[6], (DEC_BATCH, D_MODEL)),
        'c_ctx': nrm(ks[7], (D_MODEL,)),
        'w_mod': nrm(ks[8], (DEPTH, D_MODEL, 6 * D_MODEL), 0.5 * D_MODEL ** -0.5),
        'b_mod': nrm(ks[9], (DEPTH, 6 * D_MODEL), 0.01),
        'norm1_g': gain(ks[10], (DEPTH, D_MODEL)),
        'w_in': nrm(ks[11], (DEPTH, D_MODEL, IN_COLS), D_MODEL ** -0.5),
        'q_norm_g': gain(ks[12], (DEPTH, Q_LORA)),
        'kv_norm_g': gain(ks[13], (DEPTH, KV_LORA)),
        'w_uq': nrm(ks[14], (DEPTH, Q_LORA, H_A * (NOPE_DIM + ROPE_DIM)), Q_LORA ** -0.5),
        'w_uk': nrm(ks[15], (DEPTH, KV_LORA, H_A * NOPE_DIM), KV_LORA ** -0.5),
        'w_uv': nrm(ks[16], (DEPTH, KV_LORA, H_A * V_DIM), KV_LORA ** -0.5),
        'rpb': nrm(ks[17], (DEPTH, H_B, 2 * WIN_R_MAX - 1, 2 * WIN_C - 1), 0.5),
        'w_oa': nrm(ks[18], (DEPTH, H_A * V_DIM, D_MODEL), (H_A * V_DIM) ** -0.5),
        'w_ob': nrm(ks[19], (DEPTH, NA_W, D_MODEL), NA_W ** -0.5),
        'w_out': nrm(ks[20], (DEPTH, D_MODEL, D_MODEL), D_MODEL ** -0.5),
        'norm2_g': gain(ks[21], (DEPTH, D_MODEL)),
        'w_gu': nrm(ks[22], (DEPTH, D_MODEL, 2 * D_FF), D_MODEL ** -0.5),
        'w_down': nrm(ks[23], (DEPTH, D_FF, D_MODEL), D_FF ** -0.5),
        'norm_f_g': gain(ks[24], (D_MODEL,)),
    }


def reference(x_prompt, x_sample, cache_mla_ckv, cache_mla_krope, cache_na_k, cache_na_v, c, c_ctx,
              w_mod, b_mod, norm1_g, w_in, q_norm_g, kv_norm_g, w_uq, w_uk, w_uv, rpb,
              w_oa, w_ob, w_out, norm2_g, w_gu, w_down, norm_f_g):
    s2 = x_sample.shape[1]
    rows = s2 // GRID_W
    cos, sin = axial_rope_tables(s2)
    xp, xs = x_prompt, x_sample
    ckv_list, krope_list, nak_list, nav_list = [], [], [], []
    for l in range(DEPTH):
        sh1, sc1, gt1, sh2, sc2, gt2 = adaln_params(c_ctx, w_mod[l], b_mod[l])
        h = modulate(rmsnorm(xp, norm1_g[l]), sh1, sc1)
        c_q, c_kv, k_rope, q_b, k_b, v_b, g_a, g_b = split_projection(h @ w_in[l])
        c_kv = rmsnorm(c_kv, kv_norm_g[l])
        q_a = mla_queries(c_q, q_norm_g[l], w_uq[l])
        k_a, v_a = mla_keys_values(c_kv, k_rope, w_uk[l], w_uv[l])
        o_a = dense_attention(q_a, k_a, v_a, MLA_SCALE)
        k_b, v_b = split_heads(k_b, H_B), split_heads(v_b, H_B)
        o_b = dense_attention(split_heads(q_b, H_B), k_b, v_b, NA_SCALE)
        xp = xp + gt1 * merge_branches(o_a, o_b, g_a, g_b, w_oa[l], w_ob[l], w_out[l])
        xp = xp + gt2 * swiglu(modulate(rmsnorm(xp, norm2_g[l]), sh2, sc2), w_gu[l], w_down[l])
        ckv_list.append(c_kv)
        krope_list.append(k_rope)
        nak_list.append(k_b)
        nav_list.append(v_b)

        sh1, sc1, gt1, sh2, sc2, gt2 = [m[:, None, :] for m in adaln_params(c, w_mod[l], b_mod[l])]
        h = modulate(rmsnorm(xs, norm1_g[l]), sh1, sc1)
        c_q, c_kv, k_rope, q_b, k_b, v_b, g_a, g_b = split_projection(h @ w_in[l])
        c_kv = rmsnorm(c_kv, kv_norm_g[l])
        q_a = mla_queries(c_q, q_norm_g[l], w_uq[l])
        q_a = jnp.concatenate([q_a[..., :NOPE_DIM], apply_rope(q_a[..., NOPE_DIM:], cos[:, None, :], sin[:, None, :])], axis=-1)
        k_a, v_a = mla_keys_values(c_kv, apply_rope(k_rope, cos, sin), w_uk[l], w_uv[l])
        k_ctx_a, v_ctx_a = mla_keys_values(cache_mla_ckv[:, l], cache_mla_krope[:, l], w_uk[l], w_uv[l])
        o_a = dense_attention(q_a, jnp.concatenate([k_a, k_ctx_a], axis=1), jnp.concatenate([v_a, v_ctx_a], axis=1), MLA_SCALE)
        o_b = neighborhood_attention(split_heads(q_b, H_B), split_heads(k_b, H_B), split_heads(v_b, H_B),
                                     cache_na_k[:, l], cache_na_v[:, l], rpb[l], rows)
        xs = xs + gt1 * merge_branches(o_a, o_b, g_a, g_b, w_oa[l], w_ob[l], w_out[l])
        xs = xs + gt2 * swiglu(modulate(rmsnorm(xs, norm2_g[l]), sh2, sc2), w_gu[l], w_down[l])

    y_prompt = rmsnorm(xp, norm_f_g)
    y_sample = rmsnorm(xs, norm_f_g)
    new_mla_ckv = jnp.stack(ckv_list, axis=1)
    new_mla_krope = jnp.stack(krope_list, axis=1)
    new_na_k = jnp.stack(nak_list, axis=1)
    new_na_v = jnp.stack(nav_list, axis=1)
    return (y_prompt, y_sample, new_mla_ckv, new_mla_krope, new_na_k, new_na_v)
```

```python
import functools

import jax
import jax.numpy as jnp
import numpy as np
from jax import lax
from jax.experimental import pallas as pl
from jax.experimental.pallas import tpu as pltpu

F32 = jnp.float32
BF16 = jnp.bfloat16

GRID_W = 64
H_A = 8
NOPE_DIM = 128
ROPE_DIM = 64
V_DIM = 128
H_B = 8
HD_B = 128
WIN_R_MAX = 8
WIN_C = 16
ROPE_THETA = 10000.0
NORM_EPS = 1e-6
NEG_INF = -1e30
MLA_SCALE = (NOPE_DIM + ROPE_DIM) ** -0.5
NA_SCALE = HD_B ** -0.5

LANES = 128
QK_SLOT = 256
NA_QROWS = 4
NA_KROWS = NA_QROWS + WIN_R_MAX
VMEM_LIMIT = 56 * 2 ** 20


def _cp(*sem):
    return pltpu.CompilerParams(dimension_semantics=sem, vmem_limit_bytes=VMEM_LIMIT)


def _dot(a, b):
    return jnp.dot(a, b, preferred_element_type=F32)


def _dot_nt(a, b):
    return lax.dot_general(a, b, (((1,), (1,)), ((), ())), preferred_element_type=F32)


def _sigmoid(x):
    return 1.0 / (1.0 + jnp.exp(-x))


def _rms(x, g):
    return x * lax.rsqrt(jnp.mean(x * x, axis=-1, keepdims=True) + NORM_EPS) * g


def _tile(n, want):
    t = min(n, want)
    while n % t:
        t //= 2
    return t


def _adaln_kernel(c_ref, w_ref, b_ref, o_ref):
    c = c_ref[...]
    s = (c * _sigmoid(c)).astype(BF16)
    o_ref[...] = _dot(s, w_ref[...].astype(BF16)) + b_ref[...]


def _adaln(cond, w_mod, b_mod):
    r, d = cond.shape
    n = w_mod.shape[1]
    tn = _tile(n, 1024)
    return pl.pallas_call(
        _adaln_kernel,
        grid=(n // tn,),
        in_specs=[pl.BlockSpec((r, d), lambda j: (0, 0)),
                  pl.BlockSpec((d, tn), lambda j: (0, j)),
                  pl.BlockSpec((1, tn), lambda j: (0, j))],
        out_specs=pl.BlockSpec((r, tn), lambda j: (0, j)),
        out_shape=jax.ShapeDtypeStruct((r, n), F32),
        compiler_params=_cp("arbitrary"),
    )(cond, w_mod, b_mod.reshape(1, n))


def _mod_spec(d, sec, row0, grid_rank):
    if grid_rank == 2:
        return pl.BlockSpec((None, 1, d), lambda b, i: (row0 + b, 0, sec))
    return pl.BlockSpec((None, 1, d), lambda b, i, j: (row0 + b, 0, sec))


def _prenorm_kernel(x_ref, g_ref, sh_ref, sc_ref, h_ref):
    xn = _rms(x_ref[...], g_ref[...])
    h_ref[...] = (xn * (1.0 + sc_ref[...]) + sh_ref[...]).astype(BF16)


def _prenorm(x, g, mod3, row0):
    bm, sm, d = x.shape
    ts = _tile(sm, 512)
    return pl.pallas_call(
        _prenorm_kernel,
        grid=(bm, sm // ts),
        in_specs=[pl.BlockSpec((None, ts, d), lambda b, i: (b, i, 0)),
                  pl.BlockSpec((1, d), lambda b, i: (0, 0)),
                  _mod_spec(d, 0, row0, 2), _mod_spec(d, 1, row0, 2)],
        out_specs=pl.BlockSpec((None, ts, d), lambda b, i: (b, i, 0)),
        out_shape=jax.ShapeDtypeStruct((bm, sm, d), BF16),
        compiler_params=_cp("parallel", "parallel"),
    )(x, g.reshape(1, d), mod3, mod3)


def _qkv_kernel(h_ref, wq_ref, wk_ref, wv_ref, q_ref, k_ref, v_ref):
    h = h_ref[...]
    q_ref[...] = (_dot(h, wq_ref[...]) * NA_SCALE).astype(q_ref.dtype)
    k_ref[...] = _dot(h, wk_ref[...]).astype(k_ref.dtype)
    v_ref[...] = _dot(h, wv_ref[...]).astype(v_ref.dtype)


def _qkv(h, wq, wk, wv, kv_dtype):
    m, d = h.shape
    n = wq.shape[1]
    tm, tn = _tile(m, 1024), _tile(n, 512)
    wspec = pl.BlockSpec((d, tn), lambda i, j: (0, j))
    ospec = pl.BlockSpec((tm, tn), lambda i, j: (i, j))
    return pl.pallas_call(
        _qkv_kernel,
        grid=(m // tm, n // tn),
        in_specs=[pl.BlockSpec((tm, d), lambda i, j: (i, 0)), wspec, wspec, wspec],
        out_specs=[ospec, ospec, ospec],
        out_shape=[jax.ShapeDtypeStruct((m, n), BF16),
                   jax.ShapeDtypeStruct((m, n), kv_dtype),
                   jax.ShapeDtypeStruct((m, n), kv_dtype)],
        compiler_params=_cp("parallel", "arbitrary"),
    )(h, wq, wk, wv)


def _rope128(g, c, s1, s2):
    return g * c + pltpu.roll(g, 96, 1) * s1 + pltpu.roll(g, 32, 1) * s2


def _latent_kernel(*refs, q_lora, kv_lora, rope):
    if rope:
        (h_ref, wl_ref, qg_ref, kg_ref, wuq_ref, c_ref, s1_ref, s2_ref,
         q_ref, ckv_ref, kr_ref) = refs
    else:
        h_ref, wl_ref, qg_ref, kg_ref, wuq_ref, q_ref, ckv_ref, kr_ref = refs
    lat = _dot(h_ref[...], wl_ref[...])
    cq = lat[:, :q_lora]
    ckv = lat[:, q_lora:q_lora + kv_lora]
    kr = lat[:, q_lora + kv_lora:]
    ckv_ref[...] = _rms(ckv, kg_ref[...])
    if rope:
        c, s1, s2 = c_ref[...], s1_ref[...], s2_ref[...]
        kr = _rope128(kr, c, s1, s2)
    kr_ref[...] = kr
    qa = _dot(_rms(cq, qg_ref[...]).astype(BF16), wuq_ref[...])
    for hh in range(H_A):
        lo = hh * QK_SLOT
        q_ref[:, lo:lo + LANES] = (qa[:, lo:lo + LANES] * MLA_SCALE).astype(BF16)
        g = qa[:, lo + LANES:lo + QK_SLOT]
        if rope:
            g = _rope128(g, c, s1, s2)
        q_ref[:, lo + LANES:lo + QK_SLOT] = (g * MLA_SCALE).astype(BF16)


def _latent(h, w_lat, q_norm_g, kv_norm_g, w_uq_p, rope_tabs):
    bm, sm, d = h.shape
    q_lora, kv_lora = q_norm_g.shape[-1], kv_norm_g.shape[-1]
    nl = w_lat.shape[1]
    ts = _tile(sm, 512)
    rope = rope_tabs is not None
    tok = lambda w: pl.BlockSpec((None, ts, w), lambda b, i: (b, i, 0))
    full = lambda a: pl.BlockSpec(a.shape, lambda b, i: (0, 0))
    qg, kg = q_norm_g.reshape(1, q_lora), kv_norm_g.reshape(1, kv_lora)
    in_specs = [tok(d), full(w_lat), full(qg), full(kg), full(w_uq_p)]
    args = [h, w_lat, qg, kg, w_uq_p]
    if rope:
        in_specs += [pl.BlockSpec((ts, LANES), lambda b, i: (i, 0))] * 3
        args += list(rope_tabs)
    return pl.pallas_call(
        functools.partial(_latent_kernel, q_lora=q_lora, kv_lora=kv_lora, rope=rope),
        grid=(bm, sm // ts),
        in_specs=in_specs,
        out_specs=[tok(H_A * QK_SLOT), tok(kv_lora), tok(LANES)],
        out_shape=[jax.ShapeDtypeStruct((bm, sm, H_A * QK_SLOT), BF16),
                   jax.ShapeDtypeStruct((bm, sm, kv_lora), F32),
                   jax.ShapeDtypeStruct((bm, sm, LANES), F32)],
        compiler_params=_cp("parallel", "parallel"),
    )(*args)


def _kvexp_kernel(ckv_ref, kr_ref, wuk_ref, wuv_ref, k_ref, v_ref):
    c = ckv_ref[...].astype(BF16)
    kn = _dot(c, wuk_ref[...])
    v_ref[...] = _dot(c, wuv_ref[...]).astype(BF16)
    krb = kr_ref[...].astype(BF16)
    for hh in range(H_A):
        lo = hh * QK_SLOT
        k_ref[:, lo:lo + LANES] = kn[:, hh * NOPE_DIM:(hh + 1) * NOPE_DIM].astype(BF16)
        k_ref[:, lo + LANES:lo + QK_SLOT] = krb


def _kvexp(ckv, kr128, w_uk, w_uv):
    m, kv_lora = ckv.shape
    tm = _tile(m, 512)
    row = lambda w: pl.BlockSpec((tm, w), lambda i: (i, 0))
    full = lambda a: pl.BlockSpec(a.shape, lambda i: (0, 0))
    return pl.pallas_call(
        _kvexp_kernel,
        grid=(m // tm,),
        in_specs=[row(kv_lora), row(LANES), full(w_uk), full(w_uv)],
        out_specs=[row(H_A * QK_SLOT), row(H_A * V_DIM)],
        out_shape=[jax.ShapeDtypeStruct((m, H_A * QK_SLOT), BF16),
                   jax.ShapeDtypeStruct((m, H_A * V_DIM), BF16)],
        compiler_params=_cp("parallel"),
    )(ckv, kr128, w_uk, w_uv)


def _attn_kernel(*refs, hb, dk, dv, two):
    if two:
        q_ref, k1_ref, v1_ref, k2_ref, v2_ref, o_ref = refs
    else:
        q_ref, k1_ref, v1_ref, o_ref = refs
    for j in range(hb):
        q = q_ref[:, j * dk:(j + 1) * dk]
        s1 = _dot_nt(q, k1_ref[:, j * dk:(j + 1) * dk].astype(BF16))
        m = jnp.max(s1, axis=-1, keepdims=True)
        if two:
            s2 = _dot_nt(q, k2_ref[:, j * dk:(j + 1) * dk].astype(BF16))
            m = jnp.maximum(m, jnp.max(s2, axis=-1, keepdims=True))
        p1 = jnp.exp(s1 - m)
        l = jnp.sum(p1, axis=-1, keepdims=True)
        o = _dot(p1.astype(BF16), v1_ref[:, j * dv:(j + 1) * dv].astype(BF16))
        if two:
            p2 = jnp.exp(s2 - m)
            l = l + jnp.sum(p2, axis=-1, keepdims=True)
            o = o + _dot(p2.astype(BF16), v2_ref[:, j * dv:(j + 1) * dv].astype(BF16))
        o_ref[:, j * dv:(j + 1) * dv] = (o * (1.0 / l)).astype(o_ref.dtype)


def _attention(q, k1, v1, k2, v2, *, heads, hb, tq):
    b, sq, qw = q.shape
    dk = qw // heads
    dv = v1.shape[-1] // heads
    two = k2 is not None
    tq = _tile(sq, tq)
    kv = lambda a, w: pl.BlockSpec((None, a.shape[1], hb * w), lambda bi, hg, qi: (bi, 0, hg))
    in_specs = [pl.BlockSpec((None, tq, hb * dk), lambda bi, hg, qi: (bi, qi, hg)),
                kv(k1, dk), kv(v1, dv)]
    args = [q, k1, v1]
    if two:
        in_specs += [kv(k2, dk), kv(v2, dv)]
        args += [k2, v2]
    return pl.pallas_call(
        functools.partial(_attn_kernel, hb=hb, dk=dk, dv=dv, two=two),
        grid=(b, heads // hb, sq // tq),
        in_specs=in_specs,
        out_specs=pl.BlockSpec((None, tq, hb * dv), lambda bi, hg, qi: (bi, qi, hg)),
        out_shape=jax.ShapeDtypeStruct((b, sq, heads * dv), BF16),
        compiler_params=_cp("parallel", "parallel", "arbitrary"),
    )(*args)


def _na_bias_tables(rpb, rows):
    win_r = min(WIN_R_MAX, rows)
    tabs = []
    for r_blk in (0, NA_QROWS, rows - NA_QROWS):
        kr0 = int(np.clip(r_blk - win_r // 2, 0, rows - NA_KROWS))
        r = (r_blk + np.arange(NA_QROWS))[:, None, None, None]
        qc = np.arange(GRID_W)[None, :, None, None]
        kr = (kr0 + np.arange(NA_KROWS))[None, None, :, None]
        kc = np.arange(GRID_W)[None, None, None, :]
        r0 = np.clip(r - win_r // 2, 0, rows - win_r)
        qs = np.clip(qc - WIN_C // 2, 0, GRID_W - WIN_C)
        valid = (kr >= r0) & (kr < r0 + win_r) & (kc >= qs) & (kc < qs + WIN_C)
        row_off = np.clip(kr - r + (WIN_R_MAX - 1), 0, 2 * WIN_R_MAX - 2)
        col_off = np.clip(kc - qc + (WIN_C - 1), 0, 2 * WIN_C - 2)
        shape = (NA_QROWS, GRID_W, NA_KROWS, GRID_W)
        idx = np.broadcast_to(row_off * (2 * WIN_C - 1) + col_off, shape).reshape(-1)
        valid = np.broadcast_to(valid, shape).reshape(NA_QROWS * GRID_W, NA_KROWS * GRID_W)
        flat = rpb.reshape(H_B, -1)[:, idx].reshape(H_B, NA_QROWS * GRID_W, NA_KROWS * GRID_W)
        tabs.append(jnp.where(valid[None], flat, NEG_INF))
    return jnp.stack(tabs, axis=1)


def _na_kernel(q_ref, k_ref, v_ref, kc_ref, vc_ref, bias_ref, o_ref, *, rows):
    nblk = rows // NA_QROWS
    nq, nk = NA_QROWS * GRID_W, NA_KROWS * GRID_W
    kc = kc_ref[...].astype(BF16)
    vc = vc_ref[...].astype(BF16)

    def block(t, carry):
        r_blk = t * NA_QROWS
        kr0 = jnp.clip(r_blk - min(WIN_R_MAX, rows) // 2, 0, rows - NA_KROWS)
        var = jnp.where(t == 0, 0, jnp.where(t == nblk - 1, 2, 1))
        q0 = pl.multiple_of(r_blk * GRID_W, GRID_W)
        k0 = pl.multiple_of(kr0 * GRID_W, GRID_W)
        q = q_ref[pl.ds(q0, nq), :]
        s = _dot_nt(q, k_ref[pl.ds(k0, nk), :]) + bias_ref[var]
        sc = _dot_nt(q, kc)
        m = jnp.maximum(jnp.max(s, axis=-1, keepdims=True), jnp.max(sc, axis=-1, keepdims=True))
        p = jnp.exp(s - m)
        pc = jnp.exp(sc - m)
        l = jnp.sum(p, axis=-1, keepdims=True) + jnp.sum(pc, axis=-1, keepdims=True)
        o = _dot(p.astype(BF16), v_ref[pl.ds(k0, nk), :]) + _dot(pc.astype(BF16), vc)
        o_ref[pl.ds(q0, nq), :] = (o * (1.0 / l)).astype(o_ref.dtype)
        return carry

    lax.fori_loop(0, nblk, block, 0)


def _na_attention(q, k, v, k_ctx, v_ctx, bias):
    b, s, _ = q.shape
    rows = s // GRID_W
    c = k_ctx.shape[1]
    head = lambda n: pl.BlockSpec((None, n, HD_B), lambda bi, h: (bi, 0, h))
    return pl.pallas_call(
        functools.partial(_na_kernel, rows=rows),
        grid=(b, H_B),
        in_specs=[head(s), head(s), head(s), head(c), head(c),
                  pl.BlockSpec((None,) + bias.shape[1:], lambda bi, h: (h, 0, 0, 0))],
        out_specs=head(s),
        out_shape=jax.ShapeDtypeStruct((b, s, H_B * HD_B), BF16),
        compiler_params=_cp("parallel", "arbitrary"),
    )(q, k, v, k_ctx, v_ctx, bias)


def _merge_kernel(oa_ref, ob_ref, h_ref, woa_ref, wob_ref, wga_ref, wgb_ref, m_ref):
    h = h_ref[...]
    ya = _dot(oa_ref[...], woa_ref[...])
    yb = _dot(ob_ref[...], wob_ref[...])
    ga = _sigmoid(_dot(h, wga_ref[...]))
    gb = _sigmoid(_dot(h, wgb_ref[...]))
    m_ref[...] = (ga * ya + gb * yb).astype(BF16)


def _merge(oa, ob, h, w_oa, w_ob, w_ga, w_gb):
    m, d = h.shape
    n = w_oa.shape[1]
    tm, tn = _tile(m, 1024), _tile(n, 512)
    row = lambda a: pl.BlockSpec((tm, a.shape[1]), lambda i, j: (i, 0))
    col = lambda a: pl.BlockSpec((a.shape[0], tn), lambda i, j: (0, j))
    return pl.pallas_call(
        _merge_kernel,
        grid=(m // tm, n // tn),
        in_specs=[row(oa), row(ob), row(h), col(w_oa), col(w_ob), col(w_ga), col(w_gb)],
        out_specs=pl.BlockSpec((tm, tn), lambda i, j: (i, j)),
        out_shape=jax.ShapeDtypeStruct((m, n), BF16),
        compiler_params=_cp("parallel", "arbitrary"),
    )(oa, ob, h, w_oa, w_ob, w_ga, w_gb)


def _mixout_kernel(m_ref, x_ref, w_ref, gt_ref, g_ref, sh_ref, sc_ref, x1_ref, h2_ref):
    x1 = x_ref[...] + gt_ref[...] * _dot(m_ref[...], w_ref[...])
    x1_ref[...] = x1
    h2_ref[...] = (_rms(x1, g_ref[...]) * (1.0 + sc_ref[...]) + sh_ref[...]).astype(BF16)


def _mixout(mm, x, w_out, g2, mod3, row0):
    bm, sm, d = x.shape
    ts = _tile(sm, 256)
    tok = pl.BlockSpec((None, ts, d), lambda b, i: (b, i, 0))
    return pl.pallas_call(
        _mixout_kernel,
        grid=(bm, sm // ts),
        in_specs=[tok, tok, pl.BlockSpec((d, d), lambda b, i: (0, 0)),
                  _mod_spec(d, 2, row0, 2), pl.BlockSpec((1, d), lambda b, i: (0, 0)),
                  _mod_spec(d, 3, row0, 2), _mod_spec(d, 4, row0, 2)],
        out_specs=[tok, tok],
        out_shape=[jax.ShapeDtypeStruct((bm, sm, d), F32),
                   jax.ShapeDtypeStruct((bm, sm, d), BF16)],
        compiler_params=_cp("parallel", "parallel"),
    )(mm, x, w_out, mod3, g2.reshape(1, d), mod3, mod3)


def _ffn_up_kernel(h_ref, wg_ref, wu_ref, o_ref):
    h = h_ref[...]
    g = _dot(h, wg_ref[...])
    u = _dot(h, wu_ref[...])
    o_ref[...] = (g * _sigmoid(g) * u).astype(BF16)


def _ffn_up(h2, w_gu):
    m, d = h2.shape
    d_ff = w_gu.shape[1] // 2
    tm, tn = _tile(m, 1024), _tile(d_ff, 512)
    nj = d_ff // tn
    return pl.pallas_call(
        _ffn_up_kernel,
        grid=(m // tm, nj),
        in_specs=[pl.BlockSpec((tm, d), lambda i, j: (i, 0)),
                  pl.BlockSpec((d, tn), lambda i, j: (0, j)),
                  pl.BlockSpec((d, tn), lambda i, j: (0, j + nj))],
        out_specs=pl.BlockSpec((tm, tn), lambda i, j: (i, j)),
        out_shape=jax.ShapeDtypeStruct((m, d_ff), BF16),
        compiler_params=_cp("parallel", "arbitrary"),
    )(h2, w_gu, w_gu)


def _ffn_down_kernel(a_ref, w_ref, x_ref, gt_ref, g_ref, y_ref, acc_ref):
    k = pl.program_id(2)

    @pl.when(k == 0)
    def _():
        acc_ref[...] = jnp.zeros_like(acc_ref)

    acc_ref[...] += _dot(a_ref[...], w_ref[...])

    @pl.when(k == pl.num_programs(2) - 1)
    def _():
        y_ref[...] = _rms(x_ref[...] + gt_ref[...] * acc_ref[...], g_ref[...])


def _ffn_down(hid, w_down, x1, norm_f_g, mod3, row0):
    bm, sm, d = x1.shape
    d_ff = w_down.shape[0]
    ts = _tile(sm, 512)
    tk = d_ff // 4 if d_ff % (4 * LANES) == 0 else _tile(d_ff, 512)
    tok = pl.BlockSpec((None, ts, d), lambda b, i, k: (b, i, 0))
    return pl.pallas_call(
        _ffn_down_kernel,
        grid=(bm, sm // ts, d_ff // tk),
        in_specs=[pl.BlockSpec((None, ts, tk), lambda b, i, k: (b, i, k)),
                  pl.BlockSpec((tk, d), lambda b, i, k: (k, 0)),
                  tok, _mod_spec(d, 5, row0, 3),
                  pl.BlockSpec((1, d), lambda b, i, k: (0, 0))],
        out_specs=tok,
        out_shape=jax.ShapeDtypeStruct((bm, sm, d), F32),
        scratch_shapes=[pltpu.VMEM((ts, d), F32)],
        compiler_params=_cp("parallel", "parallel", "arbitrary"),
    )(hid, w_down, x1, mod3, norm_f_g.reshape(1, d))


def _rope_tables(n_tokens):
    t = jnp.arange(n_tokens, dtype=jnp.int32)
    row = (t // GRID_W).astype(F32)
    col = (t % GRID_W).astype(F32)
    n_freq = ROPE_DIM // 4
    inv_freq = ROPE_THETA ** (-jnp.arange(n_freq, dtype=F32) / n_freq)
    ang = jnp.concatenate([row[:, None] * inv_freq, col[:, None] * inv_freq], axis=-1)
    cos, sin = jnp.cos(ang), jnp.sin(ang)
    z = jnp.zeros_like(cos)
    return (jnp.concatenate([cos, cos, z, z], axis=-1),
            jnp.concatenate([-sin, z, z, z], axis=-1),
            jnp.concatenate([z, sin, z, z], axis=-1))


def _layer(x, mod3, row0, w, rope_tabs, attend):
    bm, sm, d = x.shape
    m = bm * sm
    h = _prenorm(x, w["norm1_g"], mod3, row0)
    h2d = h.reshape(m, d)
    q_a, ckv, kr128 = _latent(h, w["w_lat"], w["q_norm_g"], w["kv_norm_g"], w["w_uq_p"], rope_tabs)
    k_a, v_a = _kvexp(ckv.reshape(m, -1), kr128.reshape(m, LANES), w["w_uk"], w["w_uv"])
    q_b, k_b, v_b = _qkv(h2d, w["w_q"], w["w_k"], w["w_v"], attend.kv_dtype)
    o_a, o_b = attend(q_a.reshape(m, -1), k_a, v_a, q_b, k_b, v_b)
    mm = _merge(o_a.reshape(m, -1), o_b.reshape(m, -1), h2d, w["w_oa"], w["w_ob"], w["w_ga"], w["w_gb"])
    x1, hn = _mixout(mm.reshape(bm, sm, d), x, w["w_out"], w["norm2_g"], mod3, row0)
    hid = _ffn_up(hn.reshape(m, d), w["w_gu"])
    x2 = _ffn_down(hid.reshape(bm, sm, -1), w["w_down"], x1, w["norm_f_g"], mod3, row0)
    return x2, ckv, kr128, k_b, v_b


class _PromptAttend:
    kv_dtype = F32

    def __init__(self, batch, seq):
        self.batch, self.seq = batch, seq

    def __call__(self, q_a, k_a, v_a, q_b, k_b, v_b):
        sh = lambda a: a.reshape(self.batch, self.seq, -1)
        o_a = _attention(sh(q_a), sh(k_a), sh(v_a), None, None, heads=H_A, hb=H_A, tq=self.seq)
        o_b = _attention(sh(q_b), sh(k_b), sh(v_b), None, None, heads=H_B, hb=H_B, tq=self.seq)
        return o_a, o_b


class _SampleAttend:
    kv_dtype = BF16

    def __init__(self, batch, seq, k_ctx_a, v_ctx_a, k_ctx_b, v_ctx_b, bias):
        self.batch, self.seq = batch, seq
        self.ctx = (k_ctx_a, v_ctx_a, k_ctx_b, v_ctx_b, bias)

    def __call__(self, q_a, k_a, v_a, q_b, k_b, v_b):
        k_ctx_a, v_ctx_a, k_ctx_b, v_ctx_b, bias = self.ctx
        sh = lambda a: a.reshape(self.batch, self.seq, -1)
        o_a = _attention(sh(q_a), sh(k_a), sh(v_a), k_ctx_a, v_ctx_a, heads=H_A, hb=1, tq=256)
        o_b = _na_attention(sh(q_b), sh(k_b), sh(v_b), k_ctx_b, v_ctx_b, bias)
        return o_a, o_b


def kernel(x_prompt, x_sample, cache_mla_ckv, cache_mla_krope, cache_na_k, cache_na_v, c, c_ctx,
           w_mod, b_mod, norm1_g, w_in, q_norm_g, kv_norm_g, w_uq, w_uk, w_uv, rpb,
           w_oa, w_ob, w_out, norm2_g, w_gu, w_down, norm_f_g):
    batch, seq, d = x_prompt.shape
    dec_batch, dec_seq, _ = x_sample.shape
    depth = w_mod.shape[0]
    assert depth == 1, "one trunk layer: the final norm is fused into the layer's last kernel"
    past = cache_mla_ckv.shape[2]
    q_lora, kv_lora = q_norm_g.shape[-1], kv_norm_g.shape[-1]
    na_w = H_B * HD_B
    rows = dec_seq // GRID_W
    assert rows % NA_QROWS == 0 and rows >= NA_KROWS

    n_cond = 1 + dec_batch
    r8 = -(-n_cond // 8) * 8
    cond = jnp.zeros((r8, d), F32).at[0].set(c_ctx).at[1:n_cond].set(c)

    l = 0
    mod3 = _adaln(cond, w_mod[l], b_mod[l]).reshape(r8, 1, 6 * d)

    wi = w_in[l]
    o = 0
    sec = {}
    for name, width in (("lat", q_lora + kv_lora + ROPE_DIM), ("q", na_w), ("k", na_w), ("v", na_w),
                        ("ga", d), ("gb", d)):
        sec[name] = wi[:, o:o + width]
        o += width
    lat_w = q_lora + kv_lora + LANES
    w_uq_p = jnp.pad(w_uq[l].reshape(q_lora, H_A, NOPE_DIM + ROPE_DIM),
                     ((0, 0), (0, 0), (0, QK_SLOT - NOPE_DIM - ROPE_DIM))).reshape(q_lora, H_A * QK_SLOT)
    w = {
        "w_lat": jnp.pad(sec["lat"], ((0, 0), (0, lat_w - sec["lat"].shape[1]))).astype(BF16),
        "w_q": sec["q"].astype(BF16), "w_k": sec["k"].astype(BF16), "w_v": sec["v"].astype(BF16),
        "w_ga": sec["ga"].astype(BF16), "w_gb": sec["gb"].astype(BF16),
        "w_uq_p": w_uq_p.astype(BF16), "w_uk": w_uk[l].astype(BF16), "w_uv": w_uv[l].astype(BF16),
        "w_oa": w_oa[l].astype(BF16), "w_ob": w_ob[l].astype(BF16), "w_out": w_out[l].astype(BF16),
        "w_gu": w_gu[l].astype(BF16), "w_down": w_down[l].astype(BF16),
        "norm1_g": norm1_g[l], "norm2_g": norm2_g[l], "q_norm_g": q_norm_g[l], "kv_norm_g": kv_norm_g[l],
        "norm_f_g": norm_f_g,
    }

    xp = x_prompt.reshape(1, batch * seq, d)
    yp, ckv_p, kr_p, k_b_p, v_b_p = _layer(xp, mod3, 0, w, None, _PromptAttend(batch, seq))

    kr_ctx = jnp.pad(cache_mla_krope[:, l], ((0, 0), (0, 0), (0, LANES - ROPE_DIM)))
    k_ctx_a, v_ctx_a = _kvexp(cache_mla_ckv[:, l].reshape(dec_batch * past, kv_lora),
                              kr_ctx.reshape(dec_batch * past, LANES), w["w_uk"], w["w_uv"])
    attend = _SampleAttend(dec_batch, dec_seq,
                           k_ctx_a.reshape(dec_batch, past, -1), v_ctx_a.reshape(dec_batch, past, -1),
                           cache_na_k[:, l].reshape(dec_batch, past, na_w),
                           cache_na_v[:, l].reshape(dec_batch, past, na_w),
                           _na_bias_tables(rpb[l], rows))
    ys, _, _, _, _ = _layer(x_sample, mod3, 1, w, _rope_tables(dec_seq), attend)

    return (yp.reshape(batch, seq, d), ys,
            ckv_p.reshape(batch, 1, seq, kv_lora),
            kr_p.reshape(batch, seq, LANES)[:, :, :ROPE_DIM].reshape(batch, 1, seq, ROPE_DIM),
            k_b_p.reshape(batch, 1, seq, H_B, HD_B),
            v_b_p.reshape(batch, 1, seq, H_B, HD_B))
```

```python
import functools

import jax
import jax.numpy as jnp
import numpy as np
from jax import lax
from jax.experimental import pallas as pl
from jax.experimental.pallas import tpu as pltpu

F32 = jnp.float32
BF16 = jnp.bfloat16

GRID_W = 64
H_A = 8
NOPE_DIM = 128
ROPE_DIM = 64
V_DIM = 128
H_B = 8
HD_B = 128
WIN_R_MAX = 8
WIN_C = 16
ROPE_THETA = 10000.0
NORM_EPS = 1e-6
NEG_INF = -1e30
LOG2E = 1.4426950408889634
MLA_QSCALE = (NOPE_DIM + ROPE_DIM) ** -0.5 * LOG2E
NA_QSCALE = HD_B ** -0.5 * LOG2E

LANES = 128
MXU_COLS = 256
QK_SLOT = 256
NA_QROWS = 4
NA_KROWS = NA_QROWS + WIN_R_MAX
VMEM_LIMIT = 56 * 2 ** 20


def _cp(*sem):
    return pltpu.CompilerParams(dimension_semantics=sem, vmem_limit_bytes=VMEM_LIMIT)


def _dot(a, b):
    return jnp.dot(a, b, preferred_element_type=F32)


def _dot_nt(a, b):
    return lax.dot_general(a, b, (((1,), (1,)), ((), ())), preferred_element_type=F32)


def _sigmoid(x):
    return 1.0 / (1.0 + jnp.exp(-x))


def _rms(x, g):
    return x * lax.rsqrt(jnp.mean(x * x, axis=-1, keepdims=True) + NORM_EPS) * g


def _tile(n, want):
    t = min(n, want)
    while n % t:
        t //= 2
    return t


def _adaln_kernel(c_ref, w_ref, b_ref, o_ref):
    c = c_ref[...]
    s = (c * _sigmoid(c)).astype(BF16)
    o_ref[...] = _dot(s, w_ref[...].astype(BF16)) + b_ref[...]


def _adaln(cond, w_mod, b_mod):
    r, d = cond.shape
    n = w_mod.shape[1]
    tn = _tile(n, 1024)
    return pl.pallas_call(
        _adaln_kernel,
        grid=(n // tn,),
        in_specs=[pl.BlockSpec((r, d), lambda j: (0, 0)),
                  pl.BlockSpec((d, tn), lambda j: (0, j)),
                  pl.BlockSpec((1, tn), lambda j: (0, j))],
        out_specs=pl.BlockSpec((r, tn), lambda j: (0, j)),
        out_shape=jax.ShapeDtypeStruct((r, n), F32),
        compiler_params=_cp("arbitrary"),
    )(cond, w_mod, b_mod.reshape(1, n))


def _mod_spec(d, sec, row0, grid_rank):
    if grid_rank == 2:
        return pl.BlockSpec((None, 1, d), lambda b, i: (row0 + b, 0, sec))
    return pl.BlockSpec((None, 1, d), lambda b, i, j: (row0 + b, 0, sec))


def _prenorm_kernel(x_ref, g_ref, sh_ref, sc_ref, h_ref):
    xn = _rms(x_ref[...], g_ref[...])
    h_ref[...] = (xn * (1.0 + sc_ref[...]) + sh_ref[...]).astype(BF16)


def _prenorm(x, g, mod3, row0):
    bm, sm, d = x.shape
    ts = _tile(sm, 512)
    return pl.pallas_call(
        _prenorm_kernel,
        grid=(bm, sm // ts),
        in_specs=[pl.BlockSpec((None, ts, d), lambda b, i: (b, i, 0)),
                  pl.BlockSpec((1, d), lambda b, i: (0, 0)),
                  _mod_spec(d, 0, row0, 2), _mod_spec(d, 1, row0, 2)],
        out_specs=pl.BlockSpec((None, ts, d), lambda b, i: (b, i, 0)),
        out_shape=jax.ShapeDtypeStruct((bm, sm, d), BF16),
        compiler_params=_cp("parallel", "parallel"),
    )(x, g.reshape(1, d), mod3, mod3)


def _qkv_kernel(h_ref, wq_ref, wk_ref, wv_ref, q_ref, k_ref, v_ref):
    h = h_ref[...]
    q_ref[...] = (_dot(h, wq_ref[...]) * NA_QSCALE).astype(q_ref.dtype)
    k_ref[...] = _dot(h, wk_ref[...]).astype(k_ref.dtype)
    v_ref[...] = _dot(h, wv_ref[...]).astype(v_ref.dtype)


def _qkv(h, wq, wk, wv, kv_dtype):
    m, d = h.shape
    n = wq.shape[1]
    tm = _tile(m, 1024)
    wspec = pl.BlockSpec((d, n), lambda i: (0, 0), pipeline_mode=pl.Buffered(1))
    ospec = pl.BlockSpec((tm, n), lambda i: (i, 0))
    return pl.pallas_call(
        _qkv_kernel,
        grid=(m // tm,),
        in_specs=[pl.BlockSpec((tm, d), lambda i: (i, 0)), wspec, wspec, wspec],
        out_specs=[ospec, ospec, ospec],
        out_shape=[jax.ShapeDtypeStruct((m, n), BF16),
                   jax.ShapeDtypeStruct((m, n), kv_dtype),
                   jax.ShapeDtypeStruct((m, n), kv_dtype)],
        compiler_params=_cp("parallel"),
    )(h, wq, wk, wv)


def _rope128(g, c, s1, s2):
    return g * c + pltpu.roll(g, 96, 1) * s1 + pltpu.roll(g, 32, 1) * s2


def _latent_kernel(*refs, q_lora, kv_lora, rope):
    if rope:
        (h_ref, wl_ref, qg_ref, kg_ref, wuq_ref, c_ref, s1_ref, s2_ref,
         q_ref, ckv_ref, kr_ref) = refs
    else:
        h_ref, wl_ref, qg_ref, kg_ref, wuq_ref, q_ref, ckv_ref, kr_ref = refs
    lat = _dot(h_ref[...], wl_ref[...])
    cq = lat[:, :q_lora]
    ckv = lat[:, q_lora:q_lora + kv_lora]
    kr = lat[:, q_lora + kv_lora:]
    ckv_ref[...] = _rms(ckv, kg_ref[...])
    if rope:
        c, s1, s2 = c_ref[...], s1_ref[...], s2_ref[...]
        kr = _rope128(kr, c, s1, s2)
    kr_ref[...] = kr
    qa = _dot(_rms(cq, qg_ref[...]).astype(BF16), wuq_ref[...])
    for hh in range(H_A):
        lo = hh * QK_SLOT
        q_ref[:, lo:lo + LANES] = (qa[:, lo:lo + LANES] * MLA_QSCALE).astype(BF16)
        g = qa[:, lo + LANES:lo + QK_SLOT]
        if rope:
            g = _rope128(g, c, s1, s2)
        q_ref[:, lo + LANES:lo + QK_SLOT] = (g * MLA_QSCALE).astype(BF16)


def _latent(h, w_lat, q_norm_g, kv_norm_g, w_uq_p, rope_tabs):
    bm, sm, d = h.shape
    q_lora, kv_lora = q_norm_g.shape[-1], kv_norm_g.shape[-1]
    nl = w_lat.shape[1]
    ts = _tile(sm, 512)
    rope = rope_tabs is not None
    tok = lambda w: pl.BlockSpec((None, ts, w), lambda b, i: (b, i, 0))
    full = lambda a: pl.BlockSpec(a.shape, lambda b, i: (0, 0))
    qg, kg = q_norm_g.reshape(1, q_lora), kv_norm_g.reshape(1, kv_lora)
    in_specs = [tok(d), full(w_lat), full(qg), full(kg), full(w_uq_p)]
    args = [h, w_lat, qg, kg, w_uq_p]
    if rope:
        in_specs += [pl.BlockSpec((ts, LANES), lambda b, i: (i, 0))] * 3
        args += list(rope_tabs)
    return pl.pallas_call(
        functools.partial(_latent_kernel, q_lora=q_lora, kv_lora=kv_lora, rope=rope),
        grid=(bm, sm // ts),
        in_specs=in_specs,
        out_specs=[tok(H_A * QK_SLOT), tok(kv_lora), tok(LANES)],
        out_shape=[jax.ShapeDtypeStruct((bm, sm, H_A * QK_SLOT), BF16),
                   jax.ShapeDtypeStruct((bm, sm, kv_lora), F32),
                   jax.ShapeDtypeStruct((bm, sm, LANES), F32)],
        compiler_params=_cp("parallel", "parallel"),
    )(*args)


def _kvexp_kernel(ckv_ref, kr_ref, wuk_ref, wuv_ref, k_ref, v_ref):
    c = ckv_ref[...].astype(BF16)
    kn = _dot(c, wuk_ref[...])
    v = _dot(c, wuv_ref[...])
    krb = kr_ref[...].astype(BF16)
    one = jnp.ones((c.shape[0], V_DIM), BF16)
    for hh in range(H_A):
        lo = hh * QK_SLOT
        k_ref[:, lo:lo + LANES] = kn[:, hh * NOPE_DIM:(hh + 1) * NOPE_DIM].astype(BF16)
        k_ref[:, lo + LANES:lo + QK_SLOT] = krb
        v_ref[:, 2 * hh * V_DIM:(2 * hh + 1) * V_DIM] = v[:, hh * V_DIM:(hh + 1) * V_DIM].astype(BF16)
        v_ref[:, (2 * hh + 1) * V_DIM:(2 * hh + 2) * V_DIM] = one


def _kvexp(ckv, kr128, w_uk, w_uv):
    m, kv_lora = ckv.shape
    tm = _tile(m, 512)
    row = lambda w: pl.BlockSpec((tm, w), lambda i: (i, 0))
    full = lambda a: pl.BlockSpec(a.shape, lambda i: (0, 0))
    return pl.pallas_call(
        _kvexp_kernel,
        grid=(m // tm,),
        in_specs=[row(kv_lora), row(LANES), full(w_uk), full(w_uv)],
        out_specs=[row(H_A * QK_SLOT), row(H_A * 2 * V_DIM)],
        out_shape=[jax.ShapeDtypeStruct((m, H_A * QK_SLOT), BF16),
                   jax.ShapeDtypeStruct((m, H_A * 2 * V_DIM), BF16)],
        compiler_params=_cp("parallel"),
    )(ckv, kr128, w_uk, w_uv)


def _attn_kernel(*refs, hb, dk, dv, vw, two, tk):
    ones = vw == 2 * dv
    if two:
        q_ref, k1_ref, v1_ref, k2_ref, v2_ref, o_ref = refs
    else:
        q_ref, k1_ref, v1_ref, o_ref = refs
    chunks = [(k1_ref, v1_ref, c * tk, tk) for c in range(k1_ref.shape[0] // tk)]
    if two:
        chunks.append((k2_ref, v2_ref, 0, k2_ref.shape[0]))
    for j in range(hb):
        q = q_ref[:, j * dk:(j + 1) * dk]

        def logits(ch):
            k_ref, _, lo, n = ch
            return _dot_nt(q, k_ref[lo:lo + n, j * dk:(j + 1) * dk].astype(BF16))

        s_next = logits(chunks[0])
        m = l = acc = None
        for i, (_, v_ref, lo, n) in enumerate(chunks):
            s = s_next
            if i + 1 < len(chunks):
                s_next = logits(chunks[i + 1])
            mc = jnp.max(s, axis=-1, keepdims=True)
            m_new = mc if m is None else jnp.maximum(m, mc)
            p = jnp.exp2(s - m_new)
            pv = _dot(p.astype(BF16), v_ref[lo:lo + n, j * vw:(j + 1) * vw].astype(BF16))
            if not ones:
                ps = jnp.sum(p, axis=-1, keepdims=True)
            if m is None:
                acc = pv
                l = None if ones else ps
            else:
                alpha = jnp.exp2(m - m_new)
                acc = alpha * acc + pv
                l = None if ones else alpha * l + ps
            m = m_new
        if ones:
            o = acc[:, :dv] * (1.0 / acc[:, dv:])
        else:
            o = acc * (1.0 / l)
        o_ref[:, j * dv:(j + 1) * dv] = o.astype(o_ref.dtype)


def _attention(q, k1, v1, k2, v2, *, heads, dv, hb, tq, tk=1024):
    b, sq, qw = q.shape
    dk = qw // heads
    vw = v1.shape[-1] // heads
    two = k2 is not None
    tq = _tile(sq, tq)
    kv = lambda a, w: pl.BlockSpec((None, a.shape[1], hb * w), lambda bi, hg, qi: (bi, 0, hg))
    in_specs = [pl.BlockSpec((None, tq, hb * dk), lambda bi, hg, qi: (bi, qi, hg)),
                kv(k1, dk), kv(v1, vw)]
    args = [q, k1, v1]
    if two:
        in_specs += [kv(k2, dk), kv(v2, vw)]
        args += [k2, v2]
    return pl.pallas_call(
        functools.partial(_attn_kernel, hb=hb, dk=dk, dv=dv, vw=vw, two=two,
                          tk=_tile(k1.shape[1], tk)),
        grid=(b, heads // hb, sq // tq),
        in_specs=in_specs,
        out_specs=pl.BlockSpec((None, tq, hb * dv), lambda bi, hg, qi: (bi, qi, hg)),
        out_shape=jax.ShapeDtypeStruct((b, sq, heads * dv), BF16),
        compiler_params=_cp("parallel", "parallel", "arbitrary"),
    )(*args)


def _na_bias_tables(rpb, rows):
    win_r = min(WIN_R_MAX, rows)
    n_roff = 2 * WIN_R_MAX - 1
    qc = np.arange(GRID_W)[:, None]
    kc = np.arange(GRID_W)[None, :]
    qs = np.clip(qc - WIN_C // 2, 0, GRID_W - WIN_C)
    col_valid = (kc >= qs) & (kc < qs + WIN_C)
    col_off = np.clip(kc - qc + (WIN_C - 1), 0, 2 * WIN_C - 2)
    e = jnp.where(col_valid, rpb[:, :, col_off] * LOG2E, NEG_INF)
    e = jnp.concatenate([e, jnp.full((H_B, 1, GRID_W, GRID_W), NEG_INF, F32)], axis=1)
    ridx = []
    for r_blk in (0, NA_QROWS, rows - NA_QROWS):
        kr0 = int(np.clip(r_blk - win_r // 2, 0, rows - NA_KROWS))
        r = (r_blk + np.arange(NA_QROWS))[:, None]
        kr = (kr0 + np.arange(NA_KROWS))[None, :]
        r0 = np.clip(r - win_r // 2, 0, rows - win_r)
        valid = (kr >= r0) & (kr < r0 + win_r)
        ridx.append(np.where(valid, kr - r + (WIN_R_MAX - 1), n_roff))
    ridx = np.stack(ridx).reshape(-1)
    t = jnp.take(e, ridx, axis=1).reshape(H_B, 3, NA_QROWS, NA_KROWS, GRID_W, GRID_W)
    return t.transpose(0, 1, 2, 4, 3, 5).reshape(H_B, 3, NA_QROWS * GRID_W, NA_KROWS * GRID_W)


def _na_kernel(q_ref, k_ref, v_ref, kc_ref, vc_ref, bias_ref, o_ref, *, rows):
    nblk = rows // NA_QROWS
    nq, nk = NA_QROWS * GRID_W, NA_KROWS * GRID_W
    kc = kc_ref[...].astype(BF16)
    vc = vc_ref[...].astype(BF16)

    def block(t, carry):
        r_blk = t * NA_QROWS
        kr0 = jnp.clip(r_blk - min(WIN_R_MAX, rows) // 2, 0, rows - NA_KROWS)
        var = jnp.where(t == 0, 0, jnp.where(t == nblk - 1, 2, 1))
        q0 = pl.multiple_of(r_blk * GRID_W, GRID_W)
        k0 = pl.multiple_of(kr0 * GRID_W, GRID_W)
        q = q_ref[pl.ds(q0, nq), :]
        s = _dot_nt(q, k_ref[pl.ds(k0, nk), :]) + bias_ref[var]
        sc = _dot_nt(q, kc)
        m = jnp.maximum(jnp.max(s, axis=-1, keepdims=True), jnp.max(sc, axis=-1, keepdims=True))
        p = jnp.exp2(s - m)
        pc = jnp.exp2(sc - m)
        l = jnp.sum(p, axis=-1, keepdims=True) + jnp.sum(pc, axis=-1, keepdims=True)
        o = _dot(p.astype(BF16), v_ref[pl.ds(k0, nk), :]) + _dot(pc.astype(BF16), vc)
        o_ref[pl.ds(q0, nq), :] = (o * (1.0 / l)).astype(o_ref.dtype)
        return carry

    lax.fori_loop(0, nblk, block, 0, unroll=2)


def _na_attention(q, k, v, k_ctx, v_ctx, bias):
    b, s, _ = q.shape
    rows = s // GRID_W
    c = k_ctx.shape[1]
    head = lambda n: pl.BlockSpec((None, n, HD_B), lambda bi, h: (bi, 0, h))
    return pl.pallas_call(
        functools.partial(_na_kernel, rows=rows),
        grid=(b, H_B),
        in_specs=[head(s), head(s), head(s), head(c), head(c),
                  pl.BlockSpec((None,) + bias.shape[1:], lambda bi, h: (h, 0, 0, 0))],
        out_specs=head(s),
        out_shape=jax.ShapeDtypeStruct((b, s, H_B * HD_B), BF16),
        compiler_params=_cp("parallel", "arbitrary"),
    )(q, k, v, k_ctx, v_ctx, bias)


def _merge_kernel(oa_ref, ob_ref, h_ref, woa_ref, wob_ref, wga_ref, wgb_ref, m_ref):
    h = h_ref[...]
    ya = _dot(oa_ref[...], woa_ref[...])
    yb = _dot(ob_ref[...], wob_ref[...])
    ga = _sigmoid(_dot(h, wga_ref[...]))
    gb = _sigmoid(_dot(h, wgb_ref[...]))
    m_ref[...] = (ga * ya + gb * yb).astype(BF16)


def _merge(oa, ob, h, w_oa, w_ob, w_ga, w_gb):
    m, d = h.shape
    n = w_oa.shape[1]
    tm, tn = _tile(m, 1024), _tile(n, 512)
    row = lambda a: pl.BlockSpec((tm, a.shape[1]), lambda i, j: (i, 0))
    col = lambda a: pl.BlockSpec((a.shape[0], tn), lambda i, j: (0, j))
    return pl.pallas_call(
        _merge_kernel,
        grid=(m // tm, n // tn),
        in_specs=[row(oa), row(ob), row(h), col(w_oa), col(w_ob), col(w_ga), col(w_gb)],
        out_specs=pl.BlockSpec((tm, tn), lambda i, j: (i, j)),
        out_shape=jax.ShapeDtypeStruct((m, n), BF16),
        compiler_params=_cp("parallel", "arbitrary"),
    )(oa, ob, h, w_oa, w_ob, w_ga, w_gb)


def _mixout_kernel(m_ref, x_ref, w_ref, gt_ref, g_ref, sh_ref, sc_ref, x1_ref, h2_ref):
    x1 = x_ref[...] + gt_ref[...] * _dot(m_ref[...], w_ref[...])
    x1_ref[...] = x1
    h2_ref[...] = (_rms(x1, g_ref[...]) * (1.0 + sc_ref[...]) + sh_ref[...]).astype(BF16)


def _mixout(mm, x, w_out, g2, mod3, row0):
    bm, sm, d = x.shape
    ts = _tile(sm, 256)
    tok = pl.BlockSpec((None, ts, d), lambda b, i: (b, i, 0))
    return pl.pallas_call(
        _mixout_kernel,
        grid=(bm, sm // ts),
        in_specs=[tok, tok, pl.BlockSpec((d, d), lambda b, i: (0, 0)),
                  _mod_spec(d, 2, row0, 2), pl.BlockSpec((1, d), lambda b, i: (0, 0)),
                  _mod_spec(d, 3, row0, 2), _mod_spec(d, 4, row0, 2)],
        out_specs=[tok, tok],
        out_shape=[jax.ShapeDtypeStruct((bm, sm, d), F32),
                   jax.ShapeDtypeStruct((bm, sm, d), BF16)],
        compiler_params=_cp("parallel", "parallel"),
    )(mm, x, w_out, mod3, g2.reshape(1, d), mod3, mod3)


def _ffn_up_kernel(h_ref, wg_ref, wu_ref, o_ref, *, row_split):
    rs = h_ref.shape[0] // row_split
    for r in range(row_split):
        h = h_ref[r * rs:(r + 1) * rs, :]
        g = _dot(h, wg_ref[...])
        u = _dot(h, wu_ref[...])
        o_ref[r * rs:(r + 1) * rs, :] = (g * _sigmoid(g) * u).astype(BF16)


def _ffn_up(h2, w_gu):
    m, d = h2.shape
    d_ff = w_gu.shape[1] // 2
    tm = _tile(m, 1024)
    tn = d_ff // 2 if d_ff % (2 * MXU_COLS) == 0 else _tile(d_ff, 512)
    nj = d_ff // tn
    wspec = lambda off: pl.BlockSpec((d, tn), lambda j, i: (0, j + off), pipeline_mode=pl.Buffered(1))
    return pl.pallas_call(
        functools.partial(_ffn_up_kernel, row_split=4 if tm % 32 == 0 else 1),
        grid=(nj, m // tm),
        in_specs=[pl.BlockSpec((tm, d), lambda j, i: (i, 0)), wspec(0), wspec(nj)],
        out_specs=pl.BlockSpec((tm, tn), lambda j, i: (i, j)),
        out_shape=jax.ShapeDtypeStruct((m, d_ff), BF16),
        compiler_params=_cp("arbitrary", "arbitrary"),
    )(h2, w_gu, w_gu)


def _ffn_down_kernel(a_ref, w_ref, x_ref, gt_ref, g_ref, y_ref, acc_ref):
    k = pl.program_id(2)

    @pl.when(k == 0)
    def _():
        acc_ref[...] = jnp.zeros_like(acc_ref)

    acc_ref[...] += _dot(a_ref[...], w_ref[...])

    @pl.when(k == pl.num_programs(2) - 1)
    def _():
        y_ref[...] = _rms(x_ref[...] + gt_ref[...] * acc_ref[...], g_ref[...])


def _ffn_down(hid, w_down, x1, norm_f_g, mod3, row0):
    bm, sm, d = x1.shape
    d_ff = w_down.shape[0]
    ts = _tile(sm, 512)
    tk = d_ff // 4 if d_ff % (4 * LANES) == 0 else _tile(d_ff, 512)
    tok = pl.BlockSpec((None, ts, d), lambda b, i, k: (b, i, 0))
    return pl.pallas_call(
        _ffn_down_kernel,
        grid=(bm, sm // ts, d_ff // tk),
        in_specs=[pl.BlockSpec((None, ts, tk), lambda b, i, k: (b, i, k)),
                  pl.BlockSpec((tk, d), lambda b, i, k: (k, 0)),
                  tok, _mod_spec(d, 5, row0, 3),
                  pl.BlockSpec((1, d), lambda b, i, k: (0, 0))],
        out_specs=tok,
        out_shape=jax.ShapeDtypeStruct((bm, sm, d), F32),
        scratch_shapes=[pltpu.VMEM((ts, d), F32)],
        compiler_params=_cp("parallel", "parallel", "arbitrary"),
    )(hid, w_down, x1, mod3, norm_f_g.reshape(1, d))


def _rope_tables(n_tokens):
    t = jnp.arange(n_tokens, dtype=jnp.int32)
    row = (t // GRID_W).astype(F32)
    col = (t % GRID_W).astype(F32)
    n_freq = ROPE_DIM // 4
    inv_freq = ROPE_THETA ** (-jnp.arange(n_freq, dtype=F32) / n_freq)
    ang = jnp.concatenate([row[:, None] * inv_freq, col[:, None] * inv_freq], axis=-1)
    cos, sin = jnp.cos(ang), jnp.sin(ang)
    z = jnp.zeros_like(cos)
    return (jnp.concatenate([cos, cos, z, z], axis=-1),
            jnp.concatenate([-sin, z, z, z], axis=-1),
            jnp.concatenate([z, sin, z, z], axis=-1))


def _layer(x, mod3, row0, w, rope_tabs, attend):
    bm, sm, d = x.shape
    m = bm * sm
    h = _prenorm(x, w["norm1_g"], mod3, row0)
    h2d = h.reshape(m, d)
    q_a, ckv, kr128 = _latent(h, w["w_lat"], w["q_norm_g"], w["kv_norm_g"], w["w_uq_p"], rope_tabs)
    k_a, v_a = _kvexp(ckv.reshape(m, -1), kr128.reshape(m, LANES), w["w_uk"], w["w_uv"])
    q_b, k_b, v_b = _qkv(h2d, w["w_q"], w["w_k"], w["w_v"], attend.kv_dtype)
    o_a, o_b = attend(q_a.reshape(m, -1), k_a, v_a, q_b, k_b, v_b)
    mm = _merge(o_a.reshape(m, -1), o_b.reshape(m, -1), h2d, w["w_oa"], w["w_ob"], w["w_ga"], w["w_gb"])
    x1, hn = _mixout(mm.reshape(bm, sm, d), x, w["w_out"], w["norm2_g"], mod3, row0)
    hid = _ffn_up(hn.reshape(m, d), w["w_gu"])
    x2 = _ffn_down(hid.reshape(bm, sm, -1), w["w_down"], x1, w["norm_f_g"], mod3, row0)
    return x2, ckv, kr128, k_b, v_b


class _PromptAttend:
    kv_dtype = F32

    def __init__(self, batch, seq):
        self.batch, self.seq = batch, seq

    def __call__(self, q_a, k_a, v_a, q_b, k_b, v_b):
        sh = lambda a: a.reshape(self.batch, self.seq, -1)
        o_a = _attention(sh(q_a), sh(k_a), sh(v_a), None, None, heads=H_A, dv=V_DIM, hb=H_A, tq=self.seq)
        o_b = _attention(sh(q_b), sh(k_b), sh(v_b), None, None, heads=H_B, dv=HD_B, hb=H_B, tq=self.seq)
        return o_a, o_b


class _SampleAttend:
    kv_dtype = BF16

    def __init__(self, batch, seq, k_ctx_a, v_ctx_a, k_ctx_b, v_ctx_b, bias):
        self.batch, self.seq = batch, seq
        self.ctx = (k_ctx_a, v_ctx_a, k_ctx_b, v_ctx_b, bias)

    def __call__(self, q_a, k_a, v_a, q_b, k_b, v_b):
        k_ctx_a, v_ctx_a, k_ctx_b, v_ctx_b, bias = self.ctx
        sh = lambda a: a.reshape(self.batch, self.seq, -1)
        o_a = _attention(sh(q_a), sh(k_a), sh(v_a), k_ctx_a, v_ctx_a, heads=H_A, dv=V_DIM, hb=1, tq=512)
        o_b = _na_attention(sh(q_b), sh(k_b), sh(v_b), k_ctx_b, v_ctx_b, bias)
        return o_a, o_b


def kernel(x_prompt, x_sample, cache_mla_ckv, cache_mla_krope, cache_na_k, cache_na_v, c, c_ctx,
           w_mod, b_mod, norm1_g, w_in, q_norm_g, kv_norm_g, w_uq, w_uk, w_uv, rpb,
           w_oa, w_ob, w_out, norm2_g, w_gu, w_down, norm_f_g):
    batch, seq, d = x_prompt.shape
    dec_batch, dec_seq, _ = x_sample.shape
    depth = w_mod.shape[0]
    assert depth == 1, "one trunk layer: the final norm is fused into the layer's last kernel"
    past = cache_mla_ckv.shape[2]
    q_lora, kv_lora = q_norm_g.shape[-1], kv_norm_g.shape[-1]
    na_w = H_B * HD_B
    rows = dec_seq // GRID_W
    assert rows % NA_QROWS == 0 and rows >= NA_KROWS

    n_cond = 1 + dec_batch
    r8 = -(-n_cond // 8) * 8
    cond = jnp.zeros((r8, d), F32).at[0].set(c_ctx).at[1:n_cond].set(c)

    l = 0
    mod3 = _adaln(cond, w_mod[l], b_mod[l]).reshape(r8, 1, 6 * d)

    wi = w_in[l]
    o = 0
    sec = {}
    for name, width in (("lat", q_lora + kv_lora + ROPE_DIM), ("q", na_w), ("k", na_w), ("v", na_w),
                        ("ga", d), ("gb", d)):
        sec[name] = wi[:, o:o + width]
        o += width
    lat_w = q_lora + kv_lora + LANES
    w_uq_p = jnp.pad(w_uq[l].reshape(q_lora, H_A, NOPE_DIM + ROPE_DIM),
                     ((0, 0), (0, 0), (0, QK_SLOT - NOPE_DIM - ROPE_DIM))).reshape(q_lora, H_A * QK_SLOT)
    w = {
        "w_lat": jnp.pad(sec["lat"], ((0, 0), (0, lat_w - sec["lat"].shape[1]))).astype(BF16),
        "w_q": sec["q"].astype(BF16), "w_k": sec["k"].astype(BF16), "w_v": sec["v"].astype(BF16),
        "w_ga": sec["ga"].astype(BF16), "w_gb": sec["gb"].astype(BF16),
        "w_uq_p": w_uq_p.astype(BF16), "w_uk": w_uk[l].astype(BF16), "w_uv": w_uv[l].astype(BF16),
        "w_oa": w_oa[l].astype(BF16), "w_ob": w_ob[l].astype(BF16), "w_out": w_out[l].astype(BF16),
        "w_gu": w_gu[l].astype(BF16), "w_down": w_down[l].astype(BF16),
        "norm1_g": norm1_g[l], "norm2_g": norm2_g[l], "q_norm_g": q_norm_g[l], "kv_norm_g": kv_norm_g[l],
        "norm_f_g": norm_f_g,
    }

    xp = x_prompt.reshape(1, batch * seq, d)
    yp, ckv_p, kr_p, k_b_p, v_b_p = _layer(xp, mod3, 0, w, None, _PromptAttend(batch, seq))

    kr_ctx = jnp.pad(cache_mla_krope[:, l], ((0, 0), (0, 0), (0, LANES - ROPE_DIM)))
    k_ctx_a, v_ctx_a = _kvexp(cache_mla_ckv[:, l].reshape(dec_batch * past, kv_lora),
                              kr_ctx.reshape(dec_batch * past, LANES), w["w_uk"], w["w_uv"])
    attend = _SampleAttend(dec_batch, dec_seq,
                           k_ctx_a.reshape(dec_batch, past, -1), v_ctx_a.reshape(dec_batch, past, -1),
                           cache_na_k[:, l].reshape(dec_batch, past, na_w),
                           cache_na_v[:, l].reshape(dec_batch, past, na_w),
                           _na_bias_tables(rpb[l], rows))
    ys, _, _, _, _ = _layer(x_sample, mod3, 1, w, _rope_tables(dec_seq), attend)

    return (yp.reshape(batch, seq, d), ys,
            ckv_p.reshape(batch, 1, seq, kv_lora),
            kr_p.reshape(batch, seq, LANES)[:, :, :ROPE_DIM].reshape(batch, 1, seq, ROPE_DIM),
            k_b_p.reshape(batch, 1, seq, H_B, HD_B),
            v_b_p.reshape(batch, 1, seq, H_B, HD_B))
```

```python
import functools

import jax
import jax.numpy as jnp
import numpy as np
from jax import lax
from jax.experimental import pallas as pl
from jax.experimental.pallas import tpu as pltpu

F32 = jnp.float32
BF16 = jnp.bfloat16

GRID_W = 64
H_A = 8
NOPE_DIM = 128
ROPE_DIM = 64
V_DIM = 128
H_B = 8
HD_B = 128
WIN_R_MAX = 8
WIN_C = 16
ROPE_THETA = 10000.0
NORM_EPS = 1e-6
NEG_INF = -1e30
LOG2E = 1.4426950408889634
MLA_QSCALE = (NOPE_DIM + ROPE_DIM) ** -0.5 * LOG2E
NA_QSCALE = HD_B ** -0.5 * LOG2E

LANES = 128
MXU_COLS = 256
QK_SLOT = 256
NA_QROWS = 4
NA_KROWS = NA_QROWS + WIN_R_MAX
VMEM_LIMIT = 56 * 2 ** 20


def _cp(*sem):
    return pltpu.CompilerParams(dimension_semantics=sem, vmem_limit_bytes=VMEM_LIMIT)


def _dot(a, b):
    return jnp.dot(a, b, preferred_element_type=F32)


def _dot_nt(a, b):
    return lax.dot_general(a, b, (((1,), (1,)), ((), ())), preferred_element_type=F32)


def _sigmoid(x):
    return 1.0 / (1.0 + jnp.exp(-x))


def _rms(x, g):
    return x * lax.rsqrt(jnp.mean(x * x, axis=-1, keepdims=True) + NORM_EPS) * g


def _tile(n, want):
    t = min(n, want)
    while n % t:
        t //= 2
    return t


def _adaln_kernel(c_ref, w_ref, b_ref, o_ref):
    c = c_ref[...]
    s = (c * _sigmoid(c)).astype(BF16)
    o_ref[...] = _dot(s, w_ref[...].astype(BF16)) + b_ref[...]


def _adaln(cond, w_mod, b_mod):
    r, d = cond.shape
    n = w_mod.shape[1]
    tn = _tile(n, 1024)
    return pl.pallas_call(
        _adaln_kernel,
        grid=(n // tn,),
        in_specs=[pl.BlockSpec((r, d), lambda j: (0, 0)),
                  pl.BlockSpec((d, tn), lambda j: (0, j)),
                  pl.BlockSpec((1, tn), lambda j: (0, j))],
        out_specs=pl.BlockSpec((r, tn), lambda j: (0, j)),
        out_shape=jax.ShapeDtypeStruct((r, n), F32),
        compiler_params=_cp("arbitrary"),
    )(cond, w_mod, b_mod.reshape(1, n))


def _mod_spec(d, sec, row0, grid_rank):
    if grid_rank == 2:
        return pl.BlockSpec((None, 1, d), lambda b, i: (row0 + b, 0, sec))
    return pl.BlockSpec((None, 1, d), lambda b, i, j: (row0 + b, 0, sec))


def _qkv_kernel(x_ref, g_ref, sh_ref, sc_ref, wq_ref, wk_ref, wv_ref,
                h_ref, q_ref, k_ref, v_ref, *, row_split, v_ones):
    rs = x_ref.shape[0] // row_split
    for r in range(row_split):
        rows = slice(r * rs, (r + 1) * rs)
        h = (_rms(x_ref[rows, :], g_ref[...]) * (1.0 + sc_ref[...]) + sh_ref[...]).astype(BF16)
        h_ref[rows, :] = h
        q_ref[rows, :] = (_dot(h, wq_ref[...]) * NA_QSCALE).astype(BF16)
        k_ref[rows, :] = _dot(h, wk_ref[...]).astype(k_ref.dtype)
        v = _dot(h, wv_ref[...])
        if v_ones:
            one = jnp.ones((rs, HD_B), BF16)
            for hh in range(H_B):
                v_ref[rows, 2 * hh * HD_B:(2 * hh + 1) * HD_B] = v[:, hh * HD_B:(hh + 1) * HD_B].astype(BF16)
                v_ref[rows, (2 * hh + 1) * HD_B:(2 * hh + 2) * HD_B] = one
        else:
            v_ref[rows, :] = v.astype(v_ref.dtype)


def _qkv(x, g, mod3, row0, wq, wk, wv, kv_dtype, v_ones):
    bm, sm, d = x.shape
    n = wq.shape[1]
    vn = 2 * n if v_ones else n
    ts = _tile(sm, 512)
    wspec = pl.BlockSpec((d, n), lambda b, i: (0, 0), pipeline_mode=pl.Buffered(1))
    tok = lambda w: pl.BlockSpec((None, ts, w), lambda b, i: (b, i, 0))
    return pl.pallas_call(
        functools.partial(_qkv_kernel, row_split=2 if ts % 16 == 0 else 1, v_ones=v_ones),
        grid=(bm, sm // ts),
        in_specs=[tok(d), pl.BlockSpec((1, d), lambda b, i: (0, 0)),
                  _mod_spec(d, 0, row0, 2), _mod_spec(d, 1, row0, 2), wspec, wspec, wspec],
        out_specs=[tok(d), tok(n), tok(n), tok(vn)],
        out_shape=[jax.ShapeDtypeStruct((bm, sm, d), BF16),
                   jax.ShapeDtypeStruct((bm, sm, n), BF16),
                   jax.ShapeDtypeStruct((bm, sm, n), kv_dtype),
                   jax.ShapeDtypeStruct((bm, sm, vn), kv_dtype)],
        compiler_params=_cp("parallel", "parallel"),
    )(x, g.reshape(1, d), mod3, mod3, wq, wk, wv)


def _rope128(g, c, s1, s2):
    return g * c + pltpu.roll(g, 96, 1) * s1 + pltpu.roll(g, 32, 1) * s2


def _latent_kernel(*refs, q_lora, kv_lora, rope):
    if rope:
        (h_ref, wl_ref, qg_ref, kg_ref, wuq_ref, c_ref, s1_ref, s2_ref,
         q_ref, ckv_ref, kr_ref) = refs
    else:
        h_ref, wl_ref, qg_ref, kg_ref, wuq_ref, q_ref, ckv_ref, kr_ref = refs
    lat = _dot(h_ref[...], wl_ref[...])
    cq = lat[:, :q_lora]
    ckv = lat[:, q_lora:q_lora + kv_lora]
    kr = lat[:, q_lora + kv_lora:]
    ckv_ref[...] = _rms(ckv, kg_ref[...])
    if rope:
        c, s1, s2 = c_ref[...], s1_ref[...], s2_ref[...]
        kr = _rope128(kr, c, s1, s2)
    kr_ref[...] = kr
    qa = _dot(_rms(cq, qg_ref[...]).astype(BF16), wuq_ref[...])
    for hh in range(H_A):
        lo = hh * QK_SLOT
        q_ref[:, lo:lo + LANES] = (qa[:, lo:lo + LANES] * MLA_QSCALE).astype(BF16)
        g = qa[:, lo + LANES:lo + QK_SLOT]
        if rope:
            g = _rope128(g, c, s1, s2)
        q_ref[:, lo + LANES:lo + QK_SLOT] = (g * MLA_QSCALE).astype(BF16)


def _latent(h, w_lat, q_norm_g, kv_norm_g, w_uq_p, rope_tabs):
    bm, sm, d = h.shape
    q_lora, kv_lora = q_norm_g.shape[-1], kv_norm_g.shape[-1]
    nl = w_lat.shape[1]
    ts = _tile(sm, 512)
    rope = rope_tabs is not None
    tok = lambda w: pl.BlockSpec((None, ts, w), lambda b, i: (b, i, 0))
    full = lambda a: pl.BlockSpec(a.shape, lambda b, i: (0, 0))
    qg, kg = q_norm_g.reshape(1, q_lora), kv_norm_g.reshape(1, kv_lora)
    in_specs = [tok(d), full(w_lat), full(qg), full(kg), full(w_uq_p)]
    args = [h, w_lat, qg, kg, w_uq_p]
    if rope:
        in_specs += [pl.BlockSpec((ts, LANES), lambda b, i: (i, 0))] * 3
        args += list(rope_tabs)
    return pl.pallas_call(
        functools.partial(_latent_kernel, q_lora=q_lora, kv_lora=kv_lora, rope=rope),
        grid=(bm, sm // ts),
        in_specs=in_specs,
        out_specs=[tok(H_A * QK_SLOT), tok(kv_lora), tok(LANES)],
        out_shape=[jax.ShapeDtypeStruct((bm, sm, H_A * QK_SLOT), BF16),
                   jax.ShapeDtypeStruct((bm, sm, kv_lora), F32),
                   jax.ShapeDtypeStruct((bm, sm, LANES), F32)],
        compiler_params=_cp("parallel", "parallel"),
    )(*args)


def _kvexp_kernel(ckv_ref, kr_ref, wuk_ref, wuv_ref, k_ref, v_ref):
    c = ckv_ref[...].astype(BF16)
    kn = _dot(c, wuk_ref[...])
    v = _dot(c, wuv_ref[...])
    krb = kr_ref[...].astype(BF16)
    one = jnp.ones((c.shape[0], V_DIM), BF16)
    for hh in range(H_A):
        lo = hh * QK_SLOT
        k_ref[:, lo:lo + LANES] = kn[:, hh * NOPE_DIM:(hh + 1) * NOPE_DIM].astype(BF16)
        k_ref[:, lo + LANES:lo + QK_SLOT] = krb
        v_ref[:, 2 * hh * V_DIM:(2 * hh + 1) * V_DIM] = v[:, hh * V_DIM:(hh + 1) * V_DIM].astype(BF16)
        v_ref[:, (2 * hh + 1) * V_DIM:(2 * hh + 2) * V_DIM] = one


def _kvexp(ckv, kr128, w_uk, w_uv):
    m, kv_lora = ckv.shape
    tm = _tile(m, 512)
    row = lambda w: pl.BlockSpec((tm, w), lambda i: (i, 0))
    full = lambda a: pl.BlockSpec(a.shape, lambda i: (0, 0))
    return pl.pallas_call(
        _kvexp_kernel,
        grid=(m // tm,),
        in_specs=[row(kv_lora), row(LANES), full(w_uk), full(w_uv)],
        out_specs=[row(H_A * QK_SLOT), row(H_A * 2 * V_DIM)],
        out_shape=[jax.ShapeDtypeStruct((m, H_A * QK_SLOT), BF16),
                   jax.ShapeDtypeStruct((m, H_A * 2 * V_DIM), BF16)],
        compiler_params=_cp("parallel"),
    )(ckv, kr128, w_uk, w_uv)


def _attn_kernel(*refs, hb, dk, dv, vw, two, tk):
    ones = vw == 2 * dv
    if two:
        q_ref, k1_ref, v1_ref, k2_ref, v2_ref, o_ref = refs
    else:
        q_ref, k1_ref, v1_ref, o_ref = refs
    chunks = [(k1_ref, v1_ref, c * tk, tk) for c in range(k1_ref.shape[0] // tk)]
    if two:
        chunks.append((k2_ref, v2_ref, 0, k2_ref.shape[0]))
    for j in range(hb):
        q = q_ref[:, j * dk:(j + 1) * dk]

        def logits(ch):
            k_ref, _, lo, n = ch
            return _dot_nt(q, k_ref[lo:lo + n, j * dk:(j + 1) * dk].astype(BF16))

        s_next = logits(chunks[0])
        m = l = acc = None
        for i, (_, v_ref, lo, n) in enumerate(chunks):
            s = s_next
            if i + 1 < len(chunks):
                s_next = logits(chunks[i + 1])
            mc = jnp.max(s, axis=-1, keepdims=True)
            m_new = mc if m is None else jnp.maximum(m, mc)
            p = jnp.exp2(s - m_new)
            pv = _dot(p.astype(BF16), v_ref[lo:lo + n, j * vw:(j + 1) * vw].astype(BF16))
            if not ones:
                ps = jnp.sum(p, axis=-1, keepdims=True)
            if m is None:
                acc = pv
                l = None if ones else ps
            else:
                alpha = jnp.exp2(m - m_new)
                acc = alpha * acc + pv
                l = None if ones else alpha * l + ps
            m = m_new
        if ones:
            o = acc[:, :dv] * (1.0 / acc[:, dv:])
        else:
            o = acc * (1.0 / l)
        o_ref[:, j * dv:(j + 1) * dv] = o.astype(o_ref.dtype)


def _attention(q, k1, v1, k2, v2, *, heads, dv, hb, tq, tk=1024):
    b, sq, qw = q.shape
    dk = qw // heads
    vw = v1.shape[-1] // heads
    two = k2 is not None
    tq = _tile(sq, tq)
    kv = lambda a, w: pl.BlockSpec((None, a.shape[1], hb * w), lambda bi, hg, qi: (bi, 0, hg))
    in_specs = [pl.BlockSpec((None, tq, hb * dk), lambda bi, hg, qi: (bi, qi, hg)),
                kv(k1, dk), kv(v1, vw)]
    args = [q, k1, v1]
    if two:
        in_specs += [kv(k2, dk), kv(v2, vw)]
        args += [k2, v2]
    return pl.pallas_call(
        functools.partial(_attn_kernel, hb=hb, dk=dk, dv=dv, vw=vw, two=two,
                          tk=_tile(k1.shape[1], tk)),
        grid=(b, heads // hb, sq // tq),
        in_specs=in_specs,
        out_specs=pl.BlockSpec((None, tq, hb * dv), lambda bi, hg, qi: (bi, qi, hg)),
        out_shape=jax.ShapeDtypeStruct((b, sq, heads * dv), BF16),
        compiler_params=_cp("parallel", "parallel", "arbitrary"),
    )(*args)


def _na_bias_tables(rpb, rows):
    win_r = min(WIN_R_MAX, rows)
    qc = np.arange(GRID_W)[:, None]
    kc = np.arange(GRID_W)[None, :]
    qs = np.clip(qc - WIN_C // 2, 0, GRID_W - WIN_C)
    col_valid = (kc >= qs) & (kc < qs + WIN_C)
    col_off = np.clip(kc - qc + (WIN_C - 1), 0, 2 * WIN_C - 2)
    onehot = (col_off[None] == np.arange(2 * WIN_C - 1)[:, None, None]).astype(np.float32)
    e = jnp.einsum("hrc,cqk->hrqk", rpb, onehot, precision=lax.Precision.HIGHEST)
    e = jnp.where(col_valid, e * LOG2E, NEG_INF)
    masked = jnp.full((H_B, GRID_W, GRID_W), NEG_INF, F32)
    variants = []
    for r_blk in (0, NA_QROWS, rows - NA_QROWS):
        kr0 = int(np.clip(r_blk - win_r // 2, 0, rows - NA_KROWS))
        q_rows = []
        for a in range(NA_QROWS):
            r = r_blk + a
            r0 = int(np.clip(r - win_r // 2, 0, rows - win_r))
            slabs = [e[:, kr - r + (WIN_R_MAX - 1)] if r0 <= kr < r0 + win_r else masked
                     for kr in range(kr0, kr0 + NA_KROWS)]
            q_rows.append(jnp.concatenate(slabs, axis=-1))
        variants.append(jnp.concatenate(q_rows, axis=-2))
    return jnp.stack(variants, axis=1)


def _na_kernel(q_ref, k_ref, v_ref, kc_ref, vc_ref, bias_ref, o_ref, *, rows):
    nblk = rows // NA_QROWS
    nq, nk = NA_QROWS * GRID_W, NA_KROWS * GRID_W
    kc = kc_ref[...].astype(BF16)
    vc = jnp.concatenate([vc_ref[...].astype(BF16), jnp.ones(vc_ref.shape, BF16)], axis=-1)

    for t in range(nblk):
        r_blk = t * NA_QROWS
        kr0 = int(np.clip(r_blk - min(WIN_R_MAX, rows) // 2, 0, rows - NA_KROWS))
        var = 0 if t == 0 else (2 if t == nblk - 1 else 1)
        q0, k0 = r_blk * GRID_W, kr0 * GRID_W
        q = q_ref[q0:q0 + nq, :]
        s = _dot_nt(q, k_ref[k0:k0 + nk, :]) + bias_ref[var]
        sc = _dot_nt(q, kc)
        m = jnp.maximum(jnp.max(s, axis=-1, keepdims=True), jnp.max(sc, axis=-1, keepdims=True))
        p = jnp.exp2(s - m)
        pc = jnp.exp2(sc - m)
        o = _dot(p.astype(BF16), v_ref[k0:k0 + nk, :]) + _dot(pc.astype(BF16), vc)
        o_ref[q0:q0 + nq, :] = (o[:, :HD_B] * (1.0 / o[:, HD_B:])).astype(o_ref.dtype)


def _na_attention(q, k, v, k_ctx, v_ctx, bias):
    b, s, _ = q.shape
    rows = s // GRID_W
    c = k_ctx.shape[1]
    head = lambda n: pl.BlockSpec((None, n, HD_B), lambda bi, h: (bi, 0, h))
    return pl.pallas_call(
        functools.partial(_na_kernel, rows=rows),
        grid=(b, H_B),
        in_specs=[head(s), head(s),
                  pl.BlockSpec((None, s, 2 * HD_B), lambda bi, h: (bi, 0, h)),
                  head(c), head(c),
                  pl.BlockSpec((None,) + bias.shape[1:], lambda bi, h: (h, 0, 0, 0))],
        out_specs=head(s),
        out_shape=jax.ShapeDtypeStruct((b, s, H_B * HD_B), BF16),
        compiler_params=_cp("parallel", "arbitrary"),
    )(q, k, v, k_ctx, v_ctx, bias)


def _merge_kernel(oa_ref, ob_ref, h_ref, woa_ref, wob_ref, wga_ref, wgb_ref, m_ref):
    h = h_ref[...]
    ya = _dot(oa_ref[...], woa_ref[...])
    yb = _dot(ob_ref[...], wob_ref[...])
    ga = _sigmoid(_dot(h, wga_ref[...]))
    gb = _sigmoid(_dot(h, wgb_ref[...]))
    m_ref[...] = (ga * ya + gb * yb).astype(BF16)


def _merge(oa, ob, h, w_oa, w_ob, w_ga, w_gb):
    m, d = h.shape
    n = w_oa.shape[1]
    tm, tn = _tile(m, 1024), _tile(n, 512)
    row = lambda a: pl.BlockSpec((tm, a.shape[1]), lambda i, j: (i, 0))
    col = lambda a: pl.BlockSpec((a.shape[0], tn), lambda i, j: (0, j))
    return pl.pallas_call(
        _merge_kernel,
        grid=(m // tm, n // tn),
        in_specs=[row(oa), row(ob), row(h), col(w_oa), col(w_ob), col(w_ga), col(w_gb)],
        out_specs=pl.BlockSpec((tm, tn), lambda i, j: (i, j)),
        out_shape=jax.ShapeDtypeStruct((m, n), BF16),
        compiler_params=_cp("parallel", "arbitrary"),
    )(oa, ob, h, w_oa, w_ob, w_ga, w_gb)


def _mixout_kernel(m_ref, x_ref, w_ref, gt_ref, g_ref, sh_ref, sc_ref, x1_ref, h2_ref):
    x1 = x_ref[...] + gt_ref[...] * _dot(m_ref[...], w_ref[...])
    x1_ref[...] = x1
    h2_ref[...] = (_rms(x1, g_ref[...]) * (1.0 + sc_ref[...]) + sh_ref[...]).astype(BF16)


def _mixout(mm, x, w_out, g2, mod3, row0):
    bm, sm, d = x.shape
    ts = _tile(sm, 512)
    tok = pl.BlockSpec((None, ts, d), lambda b, i: (b, i, 0))
    return pl.pallas_call(
        _mixout_kernel,
        grid=(bm, sm // ts),
        in_specs=[tok, tok, pl.BlockSpec((d, d), lambda b, i: (0, 0), pipeline_mode=pl.Buffered(1)),
                  _mod_spec(d, 2, row0, 2), pl.BlockSpec((1, d), lambda b, i: (0, 0)),
                  _mod_spec(d, 3, row0, 2), _mod_spec(d, 4, row0, 2)],
        out_specs=[tok, tok],
        out_shape=[jax.ShapeDtypeStruct((bm, sm, d), F32),
                   jax.ShapeDtypeStruct((bm, sm, d), BF16)],
        compiler_params=_cp("parallel", "parallel"),
    )(mm, x, w_out, mod3, g2.reshape(1, d), mod3, mod3)


def _ffn_up_kernel(h_ref, wg_ref, wu_ref, o_ref, *, row_split):
    rs = h_ref.shape[0] // row_split
    for r in range(row_split):
        h = h_ref[r * rs:(r + 1) * rs, :]
        g = _dot(h, wg_ref[...])
        u = _dot(h, wu_ref[...])
        o_ref[r * rs:(r + 1) * rs, :] = (g * _sigmoid(g) * u).astype(BF16)


def _ffn_up(h2, w_gu):
    m, d = h2.shape
    d_ff = w_gu.shape[1] // 2
    tm = _tile(m, 1024)
    tn = d_ff // 2 if d_ff % (2 * MXU_COLS) == 0 else _tile(d_ff, 512)
    nj = d_ff // tn
    wspec = lambda off: pl.BlockSpec((d, tn), lambda j, i: (0, j + off), pipeline_mode=pl.Buffered(1))
    return pl.pallas_call(
        functools.partial(_ffn_up_kernel, row_split=4 if tm % 32 == 0 else 1),
        grid=(nj, m // tm),
        in_specs=[pl.BlockSpec((tm, d), lambda j, i: (i, 0)), wspec(0), wspec(nj)],
        out_specs=pl.BlockSpec((tm, tn), lambda j, i: (i, j)),
        out_shape=jax.ShapeDtypeStruct((m, d_ff), BF16),
        compiler_params=_cp("arbitrary", "arbitrary"),
    )(h2, w_gu, w_gu)


def _ffn_down_kernel(a_ref, w_ref, x_hbm, gt_ref, g_ref, y_ref, x_buf, x_sem):
    b, i, k = pl.program_id(0), pl.program_id(1), pl.program_id(2)
    ts = y_ref.shape[0]

    def x_copy():
        return pltpu.make_async_copy(x_hbm.at[b, pl.ds(i * ts, ts), :], x_buf, x_sem)

    @pl.when(k == 0)
    def _():
        x_copy().start()
        y_ref[...] = _dot(a_ref[...], w_ref[...])

    @pl.when(k > 0)
    def _():
        y_ref[...] += _dot(a_ref[...], w_ref[...])

    @pl.when(k == pl.num_programs(2) - 1)
    def _():
        x_copy().wait()
        y_ref[...] = _rms(x_buf[...] + gt_ref[...] * y_ref[...], g_ref[...])


def _ffn_down(hid, w_down, x1, norm_f_g, mod3, row0):
    bm, sm, d = x1.shape
    d_ff = w_down.shape[0]
    ts = _tile(sm, 1024)
    tk = _tile(d_ff, 512)
    return pl.pallas_call(
        _ffn_down_kernel,
        grid=(bm, sm // ts, d_ff // tk),
        in_specs=[pl.BlockSpec((None, ts, tk), lambda b, i, k: (b, i, k)),
                  pl.BlockSpec((tk, d), lambda b, i, k: (k, 0)),
                  pl.BlockSpec(memory_space=pl.ANY),
                  _mod_spec(d, 5, row0, 3),
                  pl.BlockSpec((1, d), lambda b, i, k: (0, 0))],
        out_specs=pl.BlockSpec((None, ts, d), lambda b, i, k: (b, i, 0)),
        out_shape=jax.ShapeDtypeStruct((bm, sm, d), F32),
        scratch_shapes=[pltpu.VMEM((ts, d), F32), pltpu.SemaphoreType.DMA(())],
        compiler_params=_cp("arbitrary", "arbitrary", "arbitrary"),
    )(hid, w_down, x1, mod3, norm_f_g.reshape(1, d))


def _rope_tables(n_tokens):
    t = jnp.arange(n_tokens, dtype=jnp.int32)
    row = (t // GRID_W).astype(F32)
    col = (t % GRID_W).astype(F32)
    n_freq = ROPE_DIM // 4
    inv_freq = ROPE_THETA ** (-jnp.arange(n_freq, dtype=F32) / n_freq)
    ang = jnp.concatenate([row[:, None] * inv_freq, col[:, None] * inv_freq], axis=-1)
    cos, sin = jnp.cos(ang), jnp.sin(ang)
    z = jnp.zeros_like(cos)
    return (jnp.concatenate([cos, cos, z, z], axis=-1),
            jnp.concatenate([-sin, z, z, z], axis=-1),
            jnp.concatenate([z, sin, z, z], axis=-1))


def _layer(x, mod3, row0, w, rope_tabs, attend):
    bm, sm, d = x.shape
    m = bm * sm
    h, q_b, k_b, v_b = _qkv(x, w["norm1_g"], mod3, row0, w["w_q"], w["w_k"], w["w_v"],
                            attend.kv_dtype, attend.v_ones)
    h2d = h.reshape(m, d)
    q_a, ckv, kr128 = _latent(h, w["w_lat"], w["q_norm_g"], w["kv_norm_g"], w["w_uq_p"], rope_tabs)
    k_a, v_a = _kvexp(ckv.reshape(m, -1), kr128.reshape(m, LANES), w["w_uk"], w["w_uv"])
    o_a, o_b = attend(q_a.reshape(m, -1), k_a, v_a, q_b, k_b, v_b)
    mm = _merge(o_a.reshape(m, -1), o_b.reshape(m, -1), h2d, w["w_oa"], w["w_ob"], w["w_ga"], w["w_gb"])
    x1, hn = _mixout(mm.reshape(bm, sm, d), x, w["w_out"], w["norm2_g"], mod3, row0)
    hid = _ffn_up(hn.reshape(m, d), w["w_gu"])
    x2 = _ffn_down(hid.reshape(bm, sm, -1), w["w_down"], x1, w["norm_f_g"], mod3, row0)
    return x2, ckv, kr128, k_b, v_b


class _PromptAttend:
    kv_dtype = F32
    v_ones = False

    def __init__(self, batch, seq):
        self.batch, self.seq = batch, seq

    def __call__(self, q_a, k_a, v_a, q_b, k_b, v_b):
        sh = lambda a: a.reshape(self.batch, self.seq, -1)
        o_a = _attention(sh(q_a), sh(k_a), sh(v_a), None, None, heads=H_A, dv=V_DIM, hb=H_A, tq=self.seq)
        o_b = _attention(sh(q_b), sh(k_b), sh(v_b), None, None, heads=H_B, dv=HD_B, hb=H_B, tq=self.seq)
        return o_a, o_b


class _SampleAttend:
    kv_dtype = BF16
    v_ones = True

    def __init__(self, batch, seq, k_ctx_a, v_ctx_a, k_ctx_b, v_ctx_b, bias):
        self.batch, self.seq = batch, seq
        self.ctx = (k_ctx_a, v_ctx_a, k_ctx_b, v_ctx_b, bias)

    def __call__(self, q_a, k_a, v_a, q_b, k_b, v_b):
        k_ctx_a, v_ctx_a, k_ctx_b, v_ctx_b, bias = self.ctx
        sh = lambda a: a.reshape(self.batch, self.seq, -1)
        o_a = _attention(sh(q_a), sh(k_a), sh(v_a), k_ctx_a, v_ctx_a, heads=H_A, dv=V_DIM, hb=1, tq=512)
        o_b = _na_attention(sh(q_b), sh(k_b), sh(v_b), k_ctx_b, v_ctx_b, bias)
        return o_a, o_b


def kernel(x_prompt, x_sample, cache_mla_ckv, cache_mla_krope, cache_na_k, cache_na_v, c, c_ctx,
           w_mod, b_mod, norm1_g, w_in, q_norm_g, kv_norm_g, w_uq, w_uk, w_uv, rpb,
           w_oa, w_ob, w_out, norm2_g, w_gu, w_down, norm_f_g):
    batch, seq, d = x_prompt.shape
    dec_batch, dec_seq, _ = x_sample.shape
    depth = w_mod.shape[0]
    assert depth == 1, "one trunk layer: the final norm is fused into the layer's last kernel"
    past = cache_mla_ckv.shape[2]
    q_lora, kv_lora = q_norm_g.shape[-1], kv_norm_g.shape[-1]
    na_w = H_B * HD_B
    rows = dec_seq // GRID_W
    assert rows % NA_QROWS == 0 and rows >= NA_KROWS

    n_cond = 1 + dec_batch
    r8 = -(-n_cond // 8) * 8
    cond = jnp.zeros((r8, d), F32).at[0].set(c_ctx).at[1:n_cond].set(c)

    l = 0
    mod3 = _adaln(cond, w_mod[l], b_mod[l]).reshape(r8, 1, 6 * d)

    wi = w_in[l]
    o = 0
    sec = {}
    for name, width in (("lat", q_lora + kv_lora + ROPE_DIM), ("q", na_w), ("k", na_w), ("v", na_w),
                        ("ga", d), ("gb", d)):
        sec[name] = wi[:, o:o + width]
        o += width
    lat_w = q_lora + kv_lora + LANES
    w_uq_p = jnp.pad(w_uq[l].reshape(q_lora, H_A, NOPE_DIM + ROPE_DIM),
                     ((0, 0), (0, 0), (0, QK_SLOT - NOPE_DIM - ROPE_DIM))).reshape(q_lora, H_A * QK_SLOT)
    w = {
        "w_lat": jnp.pad(sec["lat"], ((0, 0), (0, lat_w - sec["lat"].shape[1]))).astype(BF16),
        "w_q": sec["q"].astype(BF16), "w_k": sec["k"].astype(BF16), "w_v": sec["v"].astype(BF16),
        "w_ga": sec["ga"].astype(BF16), "w_gb": sec["gb"].astype(BF16),
        "w_uq_p": w_uq_p.astype(BF16), "w_uk": w_uk[l].astype(BF16), "w_uv": w_uv[l].astype(BF16),
        "w_oa": w_oa[l].astype(BF16), "w_ob": w_ob[l].astype(BF16), "w_out": w_out[l].astype(BF16),
        "w_gu": w_gu[l].astype(BF16), "w_down": w_down[l].astype(BF16),
        "norm1_g": norm1_g[l], "norm2_g": norm2_g[l], "q_norm_g": q_norm_g[l], "kv_norm_g": kv_norm_g[l],
        "norm_f_g": norm_f_g,
    }

    xp = x_prompt.reshape(1, batch * seq, d)
    yp, ckv_p, kr_p, k_b_p, v_b_p = _layer(xp, mod3, 0, w, None, _PromptAttend(batch, seq))

    kr_ctx = jnp.pad(cache_mla_krope[:, l], ((0, 0), (0, 0), (0, LANES - ROPE_DIM)))
    k_ctx_a, v_ctx_a = _kvexp(cache_mla_ckv[:, l].reshape(dec_batch * past, kv_lora),
                              kr_ctx.reshape(dec_batch * past, LANES), w["w_uk"], w["w_uv"])
    attend = _SampleAttend(dec_batch, dec_seq,
                           k_ctx_a.reshape(dec_batch, past, -1), v_ctx_a.reshape(dec_batch, past, -1),
                           cache_na_k[:, l].reshape(dec_batch, past, na_w),
                           cache_na_v[:, l].reshape(dec_batch, past, na_w),
                           _na_bias_tables(rpb[l], rows))
    ys, _, _, _, _ = _layer(x_sample, mod3, 1, w, _rope_tables(dec_seq), attend)

    return (yp.reshape(batch, seq, d), ys,
            ckv_p.reshape(batch, 1, seq, kv_lora),
            kr_p.reshape(batch, seq, LANES)[:, :, :ROPE_DIM].reshape(batch, 1, seq, ROPE_DIM),
            k_b_p.reshape(batch, 1, seq, H_B, HD_B),
            v_b_p.reshape(batch, 1, seq, H_B, HD_B))
```

```python
import functools

import jax
import jax.numpy as jnp
import numpy as np
from jax import lax
from jax.experimental import pallas as pl
from jax.experimental.pallas import tpu as pltpu

F32 = jnp.float32
BF16 = jnp.bfloat16

GRID_W = 64
H_A = 8
NOPE_DIM = 128
ROPE_DIM = 64
V_DIM = 128
H_B = 8
HD_B = 128
WIN_R_MAX = 8
WIN_C = 16
ROPE_THETA = 10000.0
NORM_EPS = 1e-6
NEG_INF = -1e30
LOG2E = 1.4426950408889634
MLA_QSCALE = (NOPE_DIM + ROPE_DIM) ** -0.5 * LOG2E
NA_QSCALE = HD_B ** -0.5 * LOG2E

LANES = 128
MXU_COLS = 256
QK_SLOT = 256
NA_QROWS = 4
NA_KROWS = NA_QROWS + WIN_R_MAX
VMEM_LIMIT = 56 * 2 ** 20


def _cp(*sem):
    return pltpu.CompilerParams(dimension_semantics=sem, vmem_limit_bytes=VMEM_LIMIT)


def _dot(a, b):
    return jnp.dot(a, b, preferred_element_type=F32)


def _dot_nt(a, b):
    return lax.dot_general(a, b, (((1,), (1,)), ((), ())), preferred_element_type=F32)


def _sigmoid(x):
    return 1.0 / (1.0 + jnp.exp(-x))


def _rms(x, g):
    return x * lax.rsqrt(jnp.mean(x * x, axis=-1, keepdims=True) + NORM_EPS) * g


def _tile(n, want):
    t = min(n, want)
    while n % t:
        t //= 2
    return t


def _adaln_kernel(c_ref, w_ref, b_ref, o_ref):
    c = c_ref[...]
    s = (c * _sigmoid(c)).astype(BF16)
    o_ref[...] = _dot(s, w_ref[...].astype(BF16)) + b_ref[...]


def _adaln(cond, w_mod, b_mod):
    r, d = cond.shape
    n = w_mod.shape[1]
    tn = _tile(n, 1024)
    return pl.pallas_call(
        _adaln_kernel,
        grid=(n // tn,),
        in_specs=[pl.BlockSpec((r, d), lambda j: (0, 0)),
                  pl.BlockSpec((d, tn), lambda j: (0, j)),
                  pl.BlockSpec((1, tn), lambda j: (0, j))],
        out_specs=pl.BlockSpec((r, tn), lambda j: (0, j)),
        out_shape=jax.ShapeDtypeStruct((r, n), F32),
        compiler_params=_cp("arbitrary"),
    )(cond, w_mod, b_mod.reshape(1, n))


def _mod_spec(d, sec, row0, grid_rank):
    if grid_rank == 2:
        return pl.BlockSpec((None, 1, d), lambda b, i: (row0 + b, 0, sec))
    return pl.BlockSpec((None, 1, d), lambda b, i, j: (row0 + b, 0, sec))


def _qkv_kernel(x_ref, g_ref, sh_ref, sc_ref, wq_ref, wk_ref, wv_ref,
                h_ref, q_ref, k_ref, v_ref, *, row_split, v_ones):
    rs = x_ref.shape[0] // row_split
    for r in range(row_split):
        rows = slice(r * rs, (r + 1) * rs)
        h = (_rms(x_ref[rows, :], g_ref[...]) * (1.0 + sc_ref[...]) + sh_ref[...]).astype(BF16)
        h_ref[rows, :] = h
        q_ref[rows, :] = (_dot(h, wq_ref[...]) * NA_QSCALE).astype(BF16)
        k_ref[rows, :] = _dot(h, wk_ref[...]).astype(k_ref.dtype)
        v = _dot(h, wv_ref[...])
        if v_ones:
            one = jnp.ones((rs, HD_B), BF16)
            for hh in range(H_B):
                v_ref[rows, 2 * hh * HD_B:(2 * hh + 1) * HD_B] = v[:, hh * HD_B:(hh + 1) * HD_B].astype(BF16)
                v_ref[rows, (2 * hh + 1) * HD_B:(2 * hh + 2) * HD_B] = one
        else:
            v_ref[rows, :] = v.astype(v_ref.dtype)


def _qkv(x, g, mod3, row0, wq, wk, wv, kv_dtype, v_ones):
    bm, sm, d = x.shape
    n = wq.shape[1]
    vn = 2 * n if v_ones else n
    ts = _tile(sm, 512)
    wspec = pl.BlockSpec((d, n), lambda b, i: (0, 0), pipeline_mode=pl.Buffered(1))
    tok = lambda w: pl.BlockSpec((None, ts, w), lambda b, i: (b, i, 0))
    return pl.pallas_call(
        functools.partial(_qkv_kernel, row_split=2 if ts % 16 == 0 else 1, v_ones=v_ones),
        grid=(bm, sm // ts),
        in_specs=[tok(d), pl.BlockSpec((1, d), lambda b, i: (0, 0)),
                  _mod_spec(d, 0, row0, 2), _mod_spec(d, 1, row0, 2), wspec, wspec, wspec],
        out_specs=[tok(d), tok(n), tok(n), tok(vn)],
        out_shape=[jax.ShapeDtypeStruct((bm, sm, d), BF16),
                   jax.ShapeDtypeStruct((bm, sm, n), BF16),
                   jax.ShapeDtypeStruct((bm, sm, n), kv_dtype),
                   jax.ShapeDtypeStruct((bm, sm, vn), kv_dtype)],
        compiler_params=_cp("parallel", "parallel"),
    )(x, g.reshape(1, d), mod3, mod3, wq, wk, wv)


def _rope128(g, c, s1, s2):
    return g * c + pltpu.roll(g, 96, 1) * s1 + pltpu.roll(g, 32, 1) * s2


def _expand_kv(ckv, kr, wuk_ref, wuv_ref, k_ref, v_ref):
    c = ckv.astype(BF16)
    kn = _dot(c, wuk_ref[...])
    v = _dot(c, wuv_ref[...])
    krb = kr.astype(BF16)
    one = jnp.ones((c.shape[0], V_DIM), BF16)
    for hh in range(H_A):
        lo = hh * QK_SLOT
        k_ref[:, lo:lo + LANES] = kn[:, hh * NOPE_DIM:(hh + 1) * NOPE_DIM].astype(BF16)
        k_ref[:, lo + LANES:lo + QK_SLOT] = krb
        v_ref[:, 2 * hh * V_DIM:(2 * hh + 1) * V_DIM] = v[:, hh * V_DIM:(hh + 1) * V_DIM].astype(BF16)
        v_ref[:, (2 * hh + 1) * V_DIM:(2 * hh + 2) * V_DIM] = one


def _latent_kernel(*refs, q_lora, kv_lora, rope):
    if rope:
        (h_ref, wl_ref, qg_ref, kg_ref, wuq_ref, wuk_ref, wuv_ref, c_ref, s1_ref, s2_ref,
         q_ref, ckv_ref, kr_ref, k_ref, v_ref) = refs
    else:
        (h_ref, wl_ref, qg_ref, kg_ref, wuq_ref, wuk_ref, wuv_ref,
         q_ref, ckv_ref, kr_ref, k_ref, v_ref) = refs
    lat = _dot(h_ref[...], wl_ref[...])
    cq = lat[:, :q_lora]
    ckv = _rms(lat[:, q_lora:q_lora + kv_lora], kg_ref[...])
    kr = lat[:, q_lora + kv_lora:]
    ckv_ref[...] = ckv
    if rope:
        c, s1, s2 = c_ref[...], s1_ref[...], s2_ref[...]
        kr = _rope128(kr, c, s1, s2)
    kr_ref[...] = kr
    _expand_kv(ckv, kr, wuk_ref, wuv_ref, k_ref, v_ref)
    qa = _dot(_rms(cq, qg_ref[...]).astype(BF16), wuq_ref[...])
    for hh in range(H_A):
        lo = hh * QK_SLOT
        q_ref[:, lo:lo + LANES] = (qa[:, lo:lo + LANES] * MLA_QSCALE).astype(BF16)
        g = qa[:, lo + LANES:lo + QK_SLOT]
        if rope:
            g = _rope128(g, c, s1, s2)
        q_ref[:, lo + LANES:lo + QK_SLOT] = (g * MLA_QSCALE).astype(BF16)


def _latent(h, w_lat, q_norm_g, kv_norm_g, w_uq_p, w_uk, w_uv, rope_tabs):
    bm, sm, d = h.shape
    q_lora, kv_lora = q_norm_g.shape[-1], kv_norm_g.shape[-1]
    ts = _tile(sm, 512)
    rope = rope_tabs is not None
    tok = lambda w: pl.BlockSpec((None, ts, w), lambda b, i: (b, i, 0))
    full = lambda a: pl.BlockSpec(a.shape, lambda b, i: (0, 0))
    qg, kg = q_norm_g.reshape(1, q_lora), kv_norm_g.reshape(1, kv_lora)
    args = [h, w_lat, qg, kg, w_uq_p, w_uk, w_uv]
    in_specs = [tok(d)] + [full(a) for a in args[1:]]
    if rope:
        in_specs += [pl.BlockSpec((ts, LANES), lambda b, i: (i, 0))] * 3
        args += list(rope_tabs)
    widths = (H_A * QK_SLOT, kv_lora, LANES, H_A * QK_SLOT, H_A * 2 * V_DIM)
    dtypes = (BF16, F32, F32, BF16, BF16)
    return pl.pallas_call(
        functools.partial(_latent_kernel, q_lora=q_lora, kv_lora=kv_lora, rope=rope),
        grid=(bm, sm // ts),
        in_specs=in_specs,
        out_specs=[tok(w) for w in widths],
        out_shape=[jax.ShapeDtypeStruct((bm, sm, w), t) for w, t in zip(widths, dtypes)],
        compiler_params=_cp("parallel", "parallel"),
    )(*args)


def _kvexp_kernel(ckv_ref, kr_ref, wuk_ref, wuv_ref, k_ref, v_ref):
    _expand_kv(ckv_ref[...], kr_ref[...], wuk_ref, wuv_ref, k_ref, v_ref)


def _kvexp(ckv, kr128, w_uk, w_uv):
    m, kv_lora = ckv.shape
    tm = _tile(m, 512)
    row = lambda w: pl.BlockSpec((tm, w), lambda i: (i, 0))
    full = lambda a: pl.BlockSpec(a.shape, lambda i: (0, 0))
    return pl.pallas_call(
        _kvexp_kernel,
        grid=(m // tm,),
        in_specs=[row(kv_lora), row(LANES), full(w_uk), full(w_uv)],
        out_specs=[row(H_A * QK_SLOT), row(H_A * 2 * V_DIM)],
        out_shape=[jax.ShapeDtypeStruct((m, H_A * QK_SLOT), BF16),
                   jax.ShapeDtypeStruct((m, H_A * 2 * V_DIM), BF16)],
        compiler_params=_cp("parallel"),
    )(ckv, kr128, w_uk, w_uv)


def _attn_kernel(*refs, hb, dk, dv, vw, two, tk):
    ones = vw == 2 * dv
    if two:
        q_ref, k1_ref, v1_ref, k2_ref, v2_ref, o_ref = refs
    else:
        q_ref, k1_ref, v1_ref, o_ref = refs
    chunks = [(k1_ref, v1_ref, c * tk, tk) for c in range(k1_ref.shape[0] // tk)]
    if two:
        chunks.append((k2_ref, v2_ref, 0, k2_ref.shape[0]))
    for j in range(hb):
        q = q_ref[:, j * dk:(j + 1) * dk]

        def logits(ch):
            k_ref, _, lo, n = ch
            return _dot_nt(q, k_ref[lo:lo + n, j * dk:(j + 1) * dk].astype(BF16))

        s_next = logits(chunks[0])
        m = l = acc = None
        for i, (_, v_ref, lo, n) in enumerate(chunks):
            s = s_next
            if i + 1 < len(chunks):
                s_next = logits(chunks[i + 1])
            mc = jnp.max(s, axis=-1, keepdims=True)
            m_new = mc if m is None else jnp.maximum(m, mc)
            p = jnp.exp2(s - m_new)
            pv = _dot(p.astype(BF16), v_ref[lo:lo + n, j * vw:(j + 1) * vw].astype(BF16))
            if not ones:
                ps = jnp.sum(p, axis=-1, keepdims=True)
            if m is None:
                acc = pv
                l = None if ones else ps
            else:
                alpha = jnp.exp2(m - m_new)
                acc = alpha * acc + pv
                l = None if ones else alpha * l + ps
            m = m_new
        if ones:
            o = acc[:, :dv] * (1.0 / acc[:, dv:])
        else:
            o = acc * (1.0 / l)
        o_ref[:, j * dv:(j + 1) * dv] = o.astype(o_ref.dtype)


def _attention(q, k1, v1, k2, v2, *, heads, dv, hb, tq, tk=1024):
    b, sq, qw = q.shape
    dk = qw // heads
    vw = v1.shape[-1] // heads
    two = k2 is not None
    tq = _tile(sq, tq)
    kv = lambda a, w: pl.BlockSpec((None, a.shape[1], hb * w), lambda bi, hg, qi: (bi, 0, hg))
    in_specs = [pl.BlockSpec((None, tq, hb * dk), lambda bi, hg, qi: (bi, qi, hg)),
                kv(k1, dk), kv(v1, vw)]
    args = [q, k1, v1]
    if two:
        in_specs += [kv(k2, dk), kv(v2, vw)]
        args += [k2, v2]
    return pl.pallas_call(
        functools.partial(_attn_kernel, hb=hb, dk=dk, dv=dv, vw=vw, two=two,
                          tk=_tile(k1.shape[1], tk)),
        grid=(b, heads // hb, sq // tq),
        in_specs=in_specs,
        out_specs=pl.BlockSpec((None, tq, hb * dv), lambda bi, hg, qi: (bi, qi, hg)),
        out_shape=jax.ShapeDtypeStruct((b, sq, heads * dv), BF16),
        compiler_params=_cp("parallel", "parallel", "arbitrary"),
    )(*args)


def _na_bias_tables(rpb, rows):
    win_r = min(WIN_R_MAX, rows)
    qc = np.arange(GRID_W)[:, None]
    kc = np.arange(GRID_W)[None, :]
    qs = np.clip(qc - WIN_C // 2, 0, GRID_W - WIN_C)
    col_valid = (kc >= qs) & (kc < qs + WIN_C)
    col_off = np.clip(kc - qc + (WIN_C - 1), 0, 2 * WIN_C - 2)
    onehot = (col_off[None] == np.arange(2 * WIN_C - 1)[:, None, None]).astype(np.float32)
    e = jnp.einsum("hrc,cqk->hrqk", rpb, onehot, precision=lax.Precision.HIGHEST)
    e = jnp.where(col_valid, e * LOG2E, NEG_INF)
    masked = jnp.full((H_B, GRID_W, GRID_W), NEG_INF, F32)
    variants = []
    for r_blk in (0, NA_QROWS, rows - NA_QROWS):
        kr0 = int(np.clip(r_blk - win_r // 2, 0, rows - NA_KROWS))
        q_rows = []
        for a in range(NA_QROWS):
            r = r_blk + a
            r0 = int(np.clip(r - win_r // 2, 0, rows - win_r))
            slabs = [e[:, kr - r + (WIN_R_MAX - 1)] if r0 <= kr < r0 + win_r else masked
                     for kr in range(kr0, kr0 + NA_KROWS)]
            q_rows.append(jnp.concatenate(slabs, axis=-1))
        variants.append(jnp.concatenate(q_rows, axis=-2))
    return jnp.stack(variants, axis=1)


def _na_kernel(q_ref, k_ref, v_ref, kc_ref, vc_ref, bias_ref, o_ref, *, rows):
    nblk = rows // NA_QROWS
    nq, nk = NA_QROWS * GRID_W, NA_KROWS * GRID_W
    kc = kc_ref[...].astype(BF16)
    vc = jnp.concatenate([vc_ref[...].astype(BF16), jnp.ones(vc_ref.shape, BF16)], axis=-1)

    for t in range(nblk):
        r_blk = t * NA_QROWS
        kr0 = int(np.clip(r_blk - min(WIN_R_MAX, rows) // 2, 0, rows - NA_KROWS))
        var = 0 if t == 0 else (2 if t == nblk - 1 else 1)
        q0, k0 = r_blk * GRID_W, kr0 * GRID_W
        q = q_ref[q0:q0 + nq, :]
        s = _dot_nt(q, k_ref[k0:k0 + nk, :]) + bias_ref[var]
        sc = _dot_nt(q, kc)
        m = jnp.maximum(jnp.max(s, axis=-1, keepdims=True), jnp.max(sc, axis=-1, keepdims=True))
        p = jnp.exp2(s - m)
        pc = jnp.exp2(sc - m)
        o = _dot(p.astype(BF16), v_ref[k0:k0 + nk, :]) + _dot(pc.astype(BF16), vc)
        o_ref[q0:q0 + nq, :] = (o[:, :HD_B] * (1.0 / o[:, HD_B:])).astype(o_ref.dtype)


def _na_attention(q, k, v, k_ctx, v_ctx, bias):
    b, s, _ = q.shape
    rows = s // GRID_W
    c = k_ctx.shape[1]
    head = lambda n: pl.BlockSpec((None, n, HD_B), lambda bi, h: (bi, 0, h))
    return pl.pallas_call(
        functools.partial(_na_kernel, rows=rows),
        grid=(b, H_B),
        in_specs=[head(s), head(s),
                  pl.BlockSpec((None, s, 2 * HD_B), lambda bi, h: (bi, 0, h)),
                  head(c), head(c),
                  pl.BlockSpec((None,) + bias.shape[1:], lambda bi, h: (h, 0, 0, 0))],
        out_specs=head(s),
        out_shape=jax.ShapeDtypeStruct((b, s, H_B * HD_B), BF16),
        compiler_params=_cp("parallel", "arbitrary"),
    )(q, k, v, k_ctx, v_ctx, bias)


def _merge_kernel(oa_ref, ob_ref, h_ref, woa_ref, wob_ref, wga_ref, wgb_ref, m_ref, *, row_split):
    rs = h_ref.shape[0] // row_split
    for r in range(row_split):
        rows = slice(r * rs, (r + 1) * rs)
        h = h_ref[rows, :]
        ya = _dot(oa_ref[rows, :], woa_ref[...])
        yb = _dot(ob_ref[rows, :], wob_ref[...])
        ga = _sigmoid(_dot(h, wga_ref[...]))
        gb = _sigmoid(_dot(h, wgb_ref[...]))
        m_ref[rows, :] = (ga * ya + gb * yb).astype(BF16)


def _merge(oa, ob, h, w_oa, w_ob, w_ga, w_gb):
    m, d = h.shape
    n = w_oa.shape[1]
    tm = _tile(m, 512)
    row = lambda a: pl.BlockSpec((tm, a.shape[1]), lambda i: (i, 0))
    full = lambda a: pl.BlockSpec(a.shape, lambda i: (0, 0), pipeline_mode=pl.Buffered(1))
    return pl.pallas_call(
        functools.partial(_merge_kernel, row_split=2 if tm % 16 == 0 else 1),
        grid=(m // tm,),
        in_specs=[row(oa), row(ob), row(h), full(w_oa), full(w_ob), full(w_ga), full(w_gb)],
        out_specs=pl.BlockSpec((tm, n), lambda i: (i, 0)),
        out_shape=jax.ShapeDtypeStruct((m, n), BF16),
        compiler_params=_cp("parallel"),
    )(oa, ob, h, w_oa, w_ob, w_ga, w_gb)


def _mixout_kernel(m_ref, x_ref, w_ref, gt_ref, g_ref, sh_ref, sc_ref, x1_ref, h2_ref):
    x1 = x_ref[...] + gt_ref[...] * _dot(m_ref[...], w_ref[...])
    x1_ref[...] = x1
    h2_ref[...] = (_rms(x1, g_ref[...]) * (1.0 + sc_ref[...]) + sh_ref[...]).astype(BF16)


def _mixout(mm, x, w_out, g2, mod3, row0):
    bm, sm, d = x.shape
    ts = _tile(sm, 512)
    tok = pl.BlockSpec((None, ts, d), lambda b, i: (b, i, 0))
    return pl.pallas_call(
        _mixout_kernel,
        grid=(bm, sm // ts),
        in_specs=[tok, tok, pl.BlockSpec((d, d), lambda b, i: (0, 0), pipeline_mode=pl.Buffered(1)),
                  _mod_spec(d, 2, row0, 2), pl.BlockSpec((1, d), lambda b, i: (0, 0)),
                  _mod_spec(d, 3, row0, 2), _mod_spec(d, 4, row0, 2)],
        out_specs=[tok, tok],
        out_shape=[jax.ShapeDtypeStruct((bm, sm, d), F32),
                   jax.ShapeDtypeStruct((bm, sm, d), BF16)],
        compiler_params=_cp("parallel", "parallel"),
    )(mm, x, w_out, mod3, g2.reshape(1, d), mod3, mod3)


def _ffn_up_kernel(h_ref, wg_ref, wu_ref, o_ref, *, row_split):
    rs = h_ref.shape[0] // row_split
    for r in range(row_split):
        h = h_ref[r * rs:(r + 1) * rs, :]
        g = _dot(h, wg_ref[...])
        u = _dot(h, wu_ref[...])
        o_ref[r * rs:(r + 1) * rs, :] = (g * _sigmoid(g) * u).astype(BF16)


def _ffn_up(h2, w_gu):
    m, d = h2.shape
    d_ff = w_gu.shape[1] // 2
    tm = _tile(m, 1024)
    tn = d_ff // 2 if d_ff % (2 * MXU_COLS) == 0 else _tile(d_ff, 512)
    nj = d_ff // tn
    wspec = lambda off: pl.BlockSpec((d, tn), lambda j, i: (0, j + off), pipeline_mode=pl.Buffered(1))
    return pl.pallas_call(
        functools.partial(_ffn_up_kernel, row_split=4 if tm % 32 == 0 else 1),
        grid=(nj, m // tm),
        in_specs=[pl.BlockSpec((tm, d), lambda j, i: (i, 0)), wspec(0), wspec(nj)],
        out_specs=pl.BlockSpec((tm, tn), lambda j, i: (i, j)),
        out_shape=jax.ShapeDtypeStruct((m, d_ff), BF16),
        compiler_params=_cp("arbitrary", "arbitrary"),
    )(h2, w_gu, w_gu)


def _ffn_down_kernel(a_ref, w_ref, x_hbm, gt_ref, g_ref, y_ref, x_buf, x_sem, *, row_split):
    b, i = pl.program_id(0), pl.program_id(1)
    ts = y_ref.shape[0]
    x_copy = pltpu.make_async_copy(x_hbm.at[b, pl.ds(i * ts, ts), :], x_buf, x_sem)
    x_copy.start()
    rs = ts // row_split
    d = y_ref.shape[1]
    tn = _tile(d, 512)
    for r in range(row_split):
        rows = slice(r * rs, (r + 1) * rs)
        for n in range(d // tn):
            cols = slice(n * tn, (n + 1) * tn)
            y_ref[rows, cols] = _dot(a_ref[rows, :], w_ref[:, cols])
        if r == 0:
            x_copy.wait()
        y_ref[rows, :] = _rms(x_buf[rows, :] + gt_ref[...] * y_ref[rows, :], g_ref[...])


def _ffn_down(hid, w_down, x1, norm_f_g, mod3, row0):
    bm, sm, d = x1.shape
    d_ff = w_down.shape[0]
    ts = _tile(sm, 512)
    return pl.pallas_call(
        functools.partial(_ffn_down_kernel, row_split=2 if ts % 16 == 0 else 1),
        grid=(bm, sm // ts),
        in_specs=[pl.BlockSpec((None, ts, d_ff), lambda b, i: (b, i, 0)),
                  pl.BlockSpec((d_ff, d), lambda b, i: (0, 0), pipeline_mode=pl.Buffered(1)),
                  pl.BlockSpec(memory_space=pl.ANY),
                  _mod_spec(d, 5, row0, 2),
                  pl.BlockSpec((1, d), lambda b, i: (0, 0))],
        out_specs=pl.BlockSpec((None, ts, d), lambda b, i: (b, i, 0)),
        out_shape=jax.ShapeDtypeStruct((bm, sm, d), F32),
        scratch_shapes=[pltpu.VMEM((ts, d), F32), pltpu.SemaphoreType.DMA(())],
        compiler_params=_cp("arbitrary", "arbitrary"),
    )(hid, w_down, x1, mod3, norm_f_g.reshape(1, d))


def _rope_tables(n_tokens):
    t = jnp.arange(n_tokens, dtype=jnp.int32)
    row = (t // GRID_W).astype(F32)
    col = (t % GRID_W).astype(F32)
    n_freq = ROPE_DIM // 4
    inv_freq = ROPE_THETA ** (-jnp.arange(n_freq, dtype=F32) / n_freq)
    ang = jnp.concatenate([row[:, None] * inv_freq, col[:, None] * inv_freq], axis=-1)
    cos, sin = jnp.cos(ang), jnp.sin(ang)
    z = jnp.zeros_like(cos)
    return (jnp.concatenate([cos, cos, z, z], axis=-1),
            jnp.concatenate([-sin, z, z, z], axis=-1),
            jnp.concatenate([z, sin, z, z], axis=-1))


def _layer(x, mod3, row0, w, rope_tabs, attend):
    bm, sm, d = x.shape
    m = bm * sm
    h, q_b, k_b, v_b = _qkv(x, w["norm1_g"], mod3, row0, w["w_q"], w["w_k"], w["w_v"],
                            attend.kv_dtype, attend.v_ones)
    h2d = h.reshape(m, d)
    q_a, ckv, kr128, k_a, v_a = _latent(h, w["w_lat"], w["q_norm_g"], w["kv_norm_g"], w["w_uq_p"],
                                        w["w_uk"], w["w_uv"], rope_tabs)
    o_a, o_b = attend(q_a, k_a, v_a, q_b, k_b, v_b)
    mm = _merge(o_a.reshape(m, -1), o_b.reshape(m, -1), h2d, w["w_oa"], w["w_ob"], w["w_ga"], w["w_gb"])
    x1, hn = _mixout(mm.reshape(bm, sm, d), x, w["w_out"], w["norm2_g"], mod3, row0)
    hid = _ffn_up(hn.reshape(m, d), w["w_gu"])
    x2 = _ffn_down(hid.reshape(bm, sm, -1), w["w_down"], x1, w["norm_f_g"], mod3, row0)
    return x2, ckv, kr128, k_b, v_b


class _PromptAttend:
    kv_dtype = F32
    v_ones = False

    def __init__(self, batch, seq):
        self.batch, self.seq = batch, seq

    def __call__(self, q_a, k_a, v_a, q_b, k_b, v_b):
        sh = lambda a: a.reshape(self.batch, self.seq, -1)
        o_a = _attention(sh(q_a), sh(k_a), sh(v_a), None, None, heads=H_A, dv=V_DIM, hb=H_A, tq=self.seq)
        o_b = _attention(sh(q_b), sh(k_b), sh(v_b), None, None, heads=H_B, dv=HD_B, hb=H_B, tq=self.seq)
        return o_a, o_b


class _SampleAttend:
    kv_dtype = BF16
    v_ones = True

    def __init__(self, batch, seq, k_ctx_a, v_ctx_a, k_ctx_b, v_ctx_b, bias):
        self.batch, self.seq = batch, seq
        self.ctx = (k_ctx_a, v_ctx_a, k_ctx_b, v_ctx_b, bias)

    def __call__(self, q_a, k_a, v_a, q_b, k_b, v_b):
        k_ctx_a, v_ctx_a, k_ctx_b, v_ctx_b, bias = self.ctx
        sh = lambda a: a.reshape(self.batch, self.seq, -1)
        o_a = _attention(sh(q_a), sh(k_a), sh(v_a), k_ctx_a, v_ctx_a, heads=H_A, dv=V_DIM, hb=1, tq=512)
        o_b = _na_attention(sh(q_b), sh(k_b), sh(v_b), k_ctx_b, v_ctx_b, bias)
        return o_a, o_b


def kernel(x_prompt, x_sample, cache_mla_ckv, cache_mla_krope, cache_na_k, cache_na_v, c, c_ctx,
           w_mod, b_mod, norm1_g, w_in, q_norm_g, kv_norm_g, w_uq, w_uk, w_uv, rpb,
           w_oa, w_ob, w_out, norm2_g, w_gu, w_down, norm_f_g):
    batch, seq, d = x_prompt.shape
    dec_batch, dec_seq, _ = x_sample.shape
    depth = w_mod.shape[0]
    assert depth == 1, "one trunk layer: the final norm is fused into the layer's last kernel"
    past = cache_mla_ckv.shape[2]
    q_lora, kv_lora = q_norm_g.shape[-1], kv_norm_g.shape[-1]
    na_w = H_B * HD_B
    rows = dec_seq // GRID_W
    assert rows % NA_QROWS == 0 and rows >= NA_KROWS

    n_cond = 1 + dec_batch
    r8 = -(-n_cond // 8) * 8
    cond = jnp.zeros((r8, d), F32).at[0].set(c_ctx).at[1:n_cond].set(c)

    l = 0
    mod3 = _adaln(cond, w_mod[l], b_mod[l]).reshape(r8, 1, 6 * d)

    wi = w_in[l]
    o = 0
    sec = {}
    for name, width in (("lat", q_lora + kv_lora + ROPE_DIM), ("q", na_w), ("k", na_w), ("v", na_w),
                        ("ga", d), ("gb", d)):
        sec[name] = wi[:, o:o + width]
        o += width
    lat_w = q_lora + kv_lora + LANES
    w_uq_p = jnp.pad(w_uq[l].reshape(q_lora, H_A, NOPE_DIM + ROPE_DIM),
                     ((0, 0), (0, 0), (0, QK_SLOT - NOPE_DIM - ROPE_DIM))).reshape(q_lora, H_A * QK_SLOT)
    w = {
        "w_lat": jnp.pad(sec["lat"], ((0, 0), (0, lat_w - sec["lat"].shape[1]))).astype(BF16),
        "w_q": sec["q"].astype(BF16), "w_k": sec["k"].astype(BF16), "w_v": sec["v"].astype(BF16),
        "w_ga": sec["ga"].astype(BF16), "w_gb": sec["gb"].astype(BF16),
        "w_uq_p": w_uq_p.astype(BF16), "w_uk": w_uk[l].astype(BF16), "w_uv": w_uv[l].astype(BF16),
        "w_oa": w_oa[l].astype(BF16), "w_ob": w_ob[l].astype(BF16), "w_out": w_out[l].astype(BF16),
        "w_gu": w_gu[l].astype(BF16), "w_down": w_down[l].astype(BF16),
        "norm1_g": norm1_g[l], "norm2_g": norm2_g[l], "q_norm_g": q_norm_g[l], "kv_norm_g": kv_norm_g[l],
        "norm_f_g": norm_f_g,
    }

    xp = x_prompt.reshape(1, batch * seq, d)
    yp, ckv_p, kr_p, k_b_p, v_b_p = _layer(xp, mod3, 0, w, None, _PromptAttend(batch, seq))

    kr_ctx = jnp.pad(cache_mla_krope[:, l], ((0, 0), (0, 0), (0, LANES - ROPE_DIM)))
    k_ctx_a, v_ctx_a = _kvexp(cache_mla_ckv[:, l].reshape(dec_batch * past, kv_lora),
                              kr_ctx.reshape(dec_batch * past, LANES), w["w_uk"], w["w_uv"])
    attend = _SampleAttend(dec_batch, dec_seq,
                           k_ctx_a.reshape(dec_batch, past, -1), v_ctx_a.reshape(dec_batch, past, -1),
                           cache_na_k[:, l].reshape(dec_batch, past, na_w),
                           cache_na_v[:, l].reshape(dec_batch, past, na_w),
                           _na_bias_tables(rpb[l], rows))
    ys, _, _, _, _ = _layer(x_sample, mod3, 1, w, _rope_tables(dec_seq), attend)

    return (yp.reshape(batch, seq, d), ys,
            ckv_p.reshape(batch, 1, seq, kv_lora),
            kr_p.reshape(batch, seq, LANES)[:, :, :ROPE_DIM].reshape(batch, 1, seq, ROPE_DIM),
            k_b_p.reshape(batch, 1, seq, H_B, HD_B),
            v_b_p.reshape(batch, 1, seq, H_B, HD_B))
```

```python
import functools

import jax
import jax.numpy as jnp
import numpy as np
from jax import lax
from jax.experimental import pallas as pl
from jax.experimental.pallas import tpu as pltpu

F32 = jnp.float32
BF16 = jnp.bfloat16

GRID_W = 64
H_A = 8
NOPE_DIM = 128
ROPE_DIM = 64
V_DIM = 128
H_B = 8
HD_B = 128
WIN_R_MAX = 8
WIN_C = 16
ROPE_THETA = 10000.0
NORM_EPS = 1e-6
NEG_INF = -1e30
LOG2E = 1.4426950408889634
MLA_QSCALE = (NOPE_DIM + ROPE_DIM) ** -0.5 * LOG2E
NA_QSCALE = HD_B ** -0.5 * LOG2E

LANES = 128
MXU_COLS = 256
QK_SLOT = 256
NA_QROWS = 4
NA_KROWS = NA_QROWS + WIN_R_MAX
VMEM_LIMIT = 56 * 2 ** 20


def _cp(*sem):
    return pltpu.CompilerParams(dimension_semantics=sem, vmem_limit_bytes=VMEM_LIMIT)


def _dot(a, b):
    return jnp.dot(a, b, preferred_element_type=F32)


def _dot_nt(a, b):
    return lax.dot_general(a, b, (((1,), (1,)), ((), ())), preferred_element_type=F32)


def _sigmoid(x):
    return 1.0 / (1.0 + jnp.exp(-x))


def _rms(x, g):
    return x * lax.rsqrt(jnp.mean(x * x, axis=-1, keepdims=True) + NORM_EPS) * g


def _tile(n, want):
    t = min(n, want)
    while n % t:
        t //= 2
    return t


def _adaln_kernel(c_ref, w_ref, b_ref, o_ref):
    c = c_ref[...]
    s = (c * _sigmoid(c)).astype(BF16)
    o_ref[...] = _dot(s, w_ref[...].astype(BF16)) + b_ref[...]


def _adaln(cond, w_mod, b_mod):
    r, d = cond.shape
    n = w_mod.shape[1]
    tn = _tile(n, 1024)
    return pl.pallas_call(
        _adaln_kernel,
        grid=(n // tn,),
        in_specs=[pl.BlockSpec((r, d), lambda j: (0, 0)),
                  pl.BlockSpec((d, tn), lambda j: (0, j)),
                  pl.BlockSpec((1, tn), lambda j: (0, j))],
        out_specs=pl.BlockSpec((r, tn), lambda j: (0, j)),
        out_shape=jax.ShapeDtypeStruct((r, n), F32),
        compiler_params=_cp("arbitrary"),
    )(cond, w_mod, b_mod.reshape(1, n))


def _mod_spec(d, sec, row0, grid_rank):
    if grid_rank == 2:
        return pl.BlockSpec((None, 1, d), lambda b, i: (row0 + b, 0, sec))
    return pl.BlockSpec((None, 1, d), lambda b, i, j: (row0 + b, 0, sec))


def _qkv_kernel(x_ref, g_ref, sh_ref, sc_ref, wq_ref, wk_ref, wv_ref,
                h_ref, q_ref, k_ref, v_ref, *, row_split, v_ones):
    rs = x_ref.shape[0] // row_split
    for r in range(row_split):
        rows = slice(r * rs, (r + 1) * rs)
        h = (_rms(x_ref[rows, :], g_ref[...]) * (1.0 + sc_ref[...]) + sh_ref[...]).astype(BF16)
        h_ref[rows, :] = h
        q_ref[rows, :] = (_dot(h, wq_ref[...]) * NA_QSCALE).astype(BF16)
        k_ref[rows, :] = _dot(h, wk_ref[...]).astype(k_ref.dtype)
        v = _dot(h, wv_ref[...])
        if v_ones:
            one = jnp.ones((rs, HD_B), BF16)
            for hh in range(H_B):
                v_ref[rows, 2 * hh * HD_B:(2 * hh + 1) * HD_B] = v[:, hh * HD_B:(hh + 1) * HD_B].astype(BF16)
                v_ref[rows, (2 * hh + 1) * HD_B:(2 * hh + 2) * HD_B] = one
        else:
            v_ref[rows, :] = v.astype(v_ref.dtype)


def _qkv(x, g, mod3, row0, wq, wk, wv, kv_dtype, v_ones):
    bm, sm, d = x.shape
    n = wq.shape[1]
    vn = 2 * n if v_ones else n
    ts = _tile(sm, 512)
    wspec = pl.BlockSpec((d, n), lambda b, i: (0, 0), pipeline_mode=pl.Buffered(1))
    tok = lambda w: pl.BlockSpec((None, ts, w), lambda b, i: (b, i, 0))
    return pl.pallas_call(
        functools.partial(_qkv_kernel, row_split=2 if ts % 16 == 0 else 1, v_ones=v_ones),
        grid=(bm, sm // ts),
        in_specs=[tok(d), pl.BlockSpec((1, d), lambda b, i: (0, 0)),
                  _mod_spec(d, 0, row0, 2), _mod_spec(d, 1, row0, 2), wspec, wspec, wspec],
        out_specs=[tok(d), tok(n), tok(n), tok(vn)],
        out_shape=[jax.ShapeDtypeStruct((bm, sm, d), BF16),
                   jax.ShapeDtypeStruct((bm, sm, n), BF16),
                   jax.ShapeDtypeStruct((bm, sm, n), kv_dtype),
                   jax.ShapeDtypeStruct((bm, sm, vn), kv_dtype)],
        compiler_params=_cp("parallel", "parallel"),
    )(x, g.reshape(1, d), mod3, mod3, wq, wk, wv)


def _rope128(g, c, s1, s2):
    return g * c + pltpu.roll(g, 96, 1) * s1 + pltpu.roll(g, 32, 1) * s2


def _expand_kv(ckv, kr, wuk_ref, wuv_ref, k_ref, v_ref):
    c = ckv.astype(BF16)
    kn = _dot(c, wuk_ref[...])
    v = _dot(c, wuv_ref[...])
    krb = kr.astype(BF16)
    one = jnp.ones((c.shape[0], V_DIM), BF16)
    for hh in range(H_A):
        lo = hh * QK_SLOT
        k_ref[:, lo:lo + LANES] = kn[:, hh * NOPE_DIM:(hh + 1) * NOPE_DIM].astype(BF16)
        k_ref[:, lo + LANES:lo + QK_SLOT] = krb
        v_ref[:, 2 * hh * V_DIM:(2 * hh + 1) * V_DIM] = v[:, hh * V_DIM:(hh + 1) * V_DIM].astype(BF16)
        v_ref[:, (2 * hh + 1) * V_DIM:(2 * hh + 2) * V_DIM] = one


def _latent_kernel(*refs, q_lora, kv_lora, rope):
    if rope:
        (h_ref, wl_ref, qg_ref, kg_ref, wuq_ref, wuk_ref, wuv_ref, c_ref, s1_ref, s2_ref,
         q_ref, ckv_ref, kr_ref, k_ref, v_ref) = refs
    else:
        (h_ref, wl_ref, qg_ref, kg_ref, wuq_ref, wuk_ref, wuv_ref,
         q_ref, ckv_ref, kr_ref, k_ref, v_ref) = refs
    lat = _dot(h_ref[...], wl_ref[...])
    cq = lat[:, :q_lora]
    ckv = _rms(lat[:, q_lora:q_lora + kv_lora], kg_ref[...])
    kr = lat[:, q_lora + kv_lora:]
    ckv_ref[...] = ckv
    if rope:
        c, s1, s2 = c_ref[...], s1_ref[...], s2_ref[...]
        kr = _rope128(kr, c, s1, s2)
    kr_ref[...] = kr
    _expand_kv(ckv, kr, wuk_ref, wuv_ref, k_ref, v_ref)
    qa = _dot(_rms(cq, qg_ref[...]).astype(BF16), wuq_ref[...])
    for hh in range(H_A):
        lo = hh * QK_SLOT
        q_ref[:, lo:lo + LANES] = (qa[:, lo:lo + LANES] * MLA_QSCALE).astype(BF16)
        g = qa[:, lo + LANES:lo + QK_SLOT]
        if rope:
            g = _rope128(g, c, s1, s2)
        q_ref[:, lo + LANES:lo + QK_SLOT] = (g * MLA_QSCALE).astype(BF16)


def _latent(h, w_lat, q_norm_g, kv_norm_g, w_uq_p, w_uk, w_uv, rope_tabs):
    bm, sm, d = h.shape
    q_lora, kv_lora = q_norm_g.shape[-1], kv_norm_g.shape[-1]
    ts = _tile(sm, 512)
    rope = rope_tabs is not None
    tok = lambda w: pl.BlockSpec((None, ts, w), lambda b, i: (b, i, 0))
    full = lambda a: pl.BlockSpec(a.shape, lambda b, i: (0, 0))
    qg, kg = q_norm_g.reshape(1, q_lora), kv_norm_g.reshape(1, kv_lora)
    args = [h, w_lat, qg, kg, w_uq_p, w_uk, w_uv]
    in_specs = [tok(d)] + [full(a) for a in args[1:]]
    if rope:
        in_specs += [pl.BlockSpec((ts, LANES), lambda b, i: (i, 0))] * 3
        args += list(rope_tabs)
    widths = (H_A * QK_SLOT, kv_lora, LANES, H_A * QK_SLOT, H_A * 2 * V_DIM)
    dtypes = (BF16, F32, F32, BF16, BF16)
    return pl.pallas_call(
        functools.partial(_latent_kernel, q_lora=q_lora, kv_lora=kv_lora, rope=rope),
        grid=(bm, sm // ts),
        in_specs=in_specs,
        out_specs=[tok(w) for w in widths],
        out_shape=[jax.ShapeDtypeStruct((bm, sm, w), t) for w, t in zip(widths, dtypes)],
        compiler_params=_cp("parallel", "parallel"),
    )(*args)


def _kvexp_kernel(ckv_ref, kr_ref, wuk_ref, wuv_ref, k_ref, v_ref):
    _expand_kv(ckv_ref[...], kr_ref[...], wuk_ref, wuv_ref, k_ref, v_ref)


def _kvexp(ckv, kr128, w_uk, w_uv):
    bm, sm, kv_lora = ckv.shape
    ts = _tile(sm, 512)
    tok = lambda w: pl.BlockSpec((None, ts, w), lambda b, i: (b, i, 0))
    full = lambda a: pl.BlockSpec(a.shape, lambda b, i: (0, 0))
    widths = (H_A * QK_SLOT, H_A * 2 * V_DIM)
    return pl.pallas_call(
        _kvexp_kernel,
        grid=(bm, sm // ts),
        in_specs=[tok(kv_lora), tok(LANES), full(w_uk), full(w_uv)],
        out_specs=[tok(w) for w in widths],
        out_shape=[jax.ShapeDtypeStruct((bm, sm, w), BF16) for w in widths],
        compiler_params=_cp("parallel", "parallel"),
    )(ckv, kr128, w_uk, w_uv)


def _attn_kernel(*refs, hb, dk, dv, vw, two, tk):
    ones = vw == 2 * dv
    if two:
        q_ref, k1_ref, v1_ref, k2_ref, v2_ref, o_ref = refs
    else:
        q_ref, k1_ref, v1_ref, o_ref = refs
    chunks = [(k1_ref, v1_ref, c * tk, tk) for c in range(k1_ref.shape[0] // tk)]
    if two:
        chunks.append((k2_ref, v2_ref, 0, k2_ref.shape[0]))
    for j in range(hb):
        q = q_ref[:, j * dk:(j + 1) * dk]

        def logits(ch):
            k_ref, _, lo, n = ch
            return _dot_nt(q, k_ref[lo:lo + n, j * dk:(j + 1) * dk].astype(BF16))

        s_next = logits(chunks[0])
        m = l = acc = None
        for i, (_, v_ref, lo, n) in enumerate(chunks):
            s = s_next
            if i + 1 < len(chunks):
                s_next = logits(chunks[i + 1])
            mc = jnp.max(s, axis=-1, keepdims=True)
            m_new = mc if m is None else jnp.maximum(m, mc)
            p = jnp.exp2(s - m_new)
            pv = _dot(p.astype(BF16), v_ref[lo:lo + n, j * vw:(j + 1) * vw].astype(BF16))
            if not ones:
                ps = jnp.sum(p, axis=-1, keepdims=True)
            if m is None:
                acc = pv
                l = None if ones else ps
            else:
                alpha = jnp.exp2(m - m_new)
                acc = alpha * acc + pv
                l = None if ones else alpha * l + ps
            m = m_new
        if ones:
            o = acc[:, :dv] * (1.0 / acc[:, dv:])
        else:
            o = acc * (1.0 / l)
        o_ref[:, j * dv:(j + 1) * dv] = o.astype(o_ref.dtype)


def _attention(q, k1, v1, k2, v2, *, heads, dv, hb, tq, tk=1024):
    b, sq, qw = q.shape
    dk = qw // heads
    vw = v1.shape[-1] // heads
    two = k2 is not None
    tq = _tile(sq, tq)
    kv = lambda a, w: pl.BlockSpec((None, a.shape[1], hb * w), lambda bi, hg, qi: (bi, 0, hg))
    in_specs = [pl.BlockSpec((None, tq, hb * dk), lambda bi, hg, qi: (bi, qi, hg)),
                kv(k1, dk), kv(v1, vw)]
    args = [q, k1, v1]
    if two:
        in_specs += [kv(k2, dk), kv(v2, vw)]
        args += [k2, v2]
    return pl.pallas_call(
        functools.partial(_attn_kernel, hb=hb, dk=dk, dv=dv, vw=vw, two=two,
                          tk=_tile(k1.shape[1], tk)),
        grid=(b, heads // hb, sq // tq),
        in_specs=in_specs,
        out_specs=pl.BlockSpec((None, tq, hb * dv), lambda bi, hg, qi: (bi, qi, hg)),
        out_shape=jax.ShapeDtypeStruct((b, sq, heads * dv), BF16),
        compiler_params=_cp("parallel", "parallel", "arbitrary"),
    )(*args)


def _na_bias_tables(rpb, rows):
    win_r = min(WIN_R_MAX, rows)
    qc = np.arange(GRID_W)[:, None]
    kc = np.arange(GRID_W)[None, :]
    qs = np.clip(qc - WIN_C // 2, 0, GRID_W - WIN_C)
    col_valid = (kc >= qs) & (kc < qs + WIN_C)
    col_off = np.clip(kc - qc + (WIN_C - 1), 0, 2 * WIN_C - 2)
    onehot = (col_off[None] == np.arange(2 * WIN_C - 1)[:, None, None]).astype(np.float32)
    e = jnp.einsum("hrc,cqk->hrqk", rpb, onehot, precision=lax.Precision.HIGHEST)
    e = jnp.where(col_valid, e * LOG2E, NEG_INF)
    n_roff = 2 * WIN_R_MAX - 1
    e = jnp.concatenate([e, jnp.full((H_B, 1, GRID_W, GRID_W), NEG_INF, F32)], axis=1)
    slab = []
    for r_blk in (0, NA_QROWS, rows - NA_QROWS):
        kr0 = int(np.clip(r_blk - win_r // 2, 0, rows - NA_KROWS))
        per_row = []
        for a in range(NA_QROWS):
            r = r_blk + a
            r0 = int(np.clip(r - win_r // 2, 0, rows - win_r))
            per_row.append([kr - r + (WIN_R_MAX - 1) if r0 <= kr < r0 + win_r else n_roff
                            for kr in range(kr0, kr0 + NA_KROWS)])
        slab.append(per_row)

    def assemble(e_ref, o_ref):
        for v, per_row in enumerate(slab):
            for a, idx in enumerate(per_row):
                o_ref[v, a * GRID_W:(a + 1) * GRID_W, :] = jnp.concatenate([e_ref[j] for j in idx], axis=-1)

    return pl.pallas_call(
        assemble,
        grid=(H_B,),
        in_specs=[pl.BlockSpec((None, n_roff + 1, GRID_W, GRID_W), lambda h: (h, 0, 0, 0))],
        out_specs=pl.BlockSpec((None, 3, NA_QROWS * GRID_W, NA_KROWS * GRID_W), lambda h: (h, 0, 0, 0)),
        out_shape=jax.ShapeDtypeStruct((H_B, 3, NA_QROWS * GRID_W, NA_KROWS * GRID_W), F32),
        compiler_params=_cp("parallel"),
    )(e)


def _na_kernel(q_ref, k_ref, v_ref, kc_ref, vc_ref, bias_ref, o_ref, *, rows):
    nblk = rows // NA_QROWS
    nq, nk = NA_QROWS * GRID_W, NA_KROWS * GRID_W
    kc = kc_ref[...].astype(BF16)
    vc = jnp.concatenate([vc_ref[...].astype(BF16), jnp.ones(vc_ref.shape, BF16)], axis=-1)

    for t in range(nblk):
        r_blk = t * NA_QROWS
        kr0 = int(np.clip(r_blk - min(WIN_R_MAX, rows) // 2, 0, rows - NA_KROWS))
        var = 0 if t == 0 else (2 if t == nblk - 1 else 1)
        q0, k0 = r_blk * GRID_W, kr0 * GRID_W
        q = q_ref[q0:q0 + nq, :]
        s = _dot_nt(q, k_ref[k0:k0 + nk, :]) + bias_ref[var]
        sc = _dot_nt(q, kc)
        m = jnp.maximum(jnp.max(s, axis=-1, keepdims=True), jnp.max(sc, axis=-1, keepdims=True))
        p = jnp.exp2(s - m)
        pc = jnp.exp2(sc - m)
        o = _dot(p.astype(BF16), v_ref[k0:k0 + nk, :]) + _dot(pc.astype(BF16), vc)
        o_ref[q0:q0 + nq, :] = (o[:, :HD_B] * (1.0 / o[:, HD_B:])).astype(o_ref.dtype)


def _na_attention(q, k, v, k_ctx, v_ctx, bias):
    b, s, _ = q.shape
    rows = s // GRID_W
    c = k_ctx.shape[1]
    head = lambda n: pl.BlockSpec((None, n, HD_B), lambda bi, h: (bi, 0, h))
    return pl.pallas_call(
        functools.partial(_na_kernel, rows=rows),
        grid=(b, H_B),
        in_specs=[head(s), head(s),
                  pl.BlockSpec((None, s, 2 * HD_B), lambda bi, h: (bi, 0, h)),
                  head(c), head(c),
                  pl.BlockSpec((None,) + bias.shape[1:], lambda bi, h: (h, 0, 0, 0))],
        out_specs=head(s),
        out_shape=jax.ShapeDtypeStruct((b, s, H_B * HD_B), BF16),
        compiler_params=_cp("parallel", "arbitrary"),
    )(q, k, v, k_ctx, v_ctx, bias)


def _merge_kernel(oa_ref, ob_ref, h_ref, woa_ref, wob_ref, wga_ref, wgb_ref, m_ref, *, row_split):
    rs = h_ref.shape[0] // row_split
    for r in range(row_split):
        rows = slice(r * rs, (r + 1) * rs)
        h = h_ref[rows, :]
        ya = _dot(oa_ref[rows, :], woa_ref[...])
        yb = _dot(ob_ref[rows, :], wob_ref[...])
        ga = _sigmoid(_dot(h, wga_ref[...]))
        gb = _sigmoid(_dot(h, wgb_ref[...]))
        m_ref[rows, :] = (ga * ya + gb * yb).astype(BF16)


def _merge(oa, ob, h, w_oa, w_ob, w_ga, w_gb):
    m, d = h.shape
    n = w_oa.shape[1]
    tm = _tile(m, 512)
    row = lambda a: pl.BlockSpec((tm, a.shape[1]), lambda i: (i, 0))
    full = lambda a: pl.BlockSpec(a.shape, lambda i: (0, 0), pipeline_mode=pl.Buffered(1))
    return pl.pallas_call(
        functools.partial(_merge_kernel, row_split=2 if tm % 16 == 0 else 1),
        grid=(m // tm,),
        in_specs=[row(oa), row(ob), row(h), full(w_oa), full(w_ob), full(w_ga), full(w_gb)],
        out_specs=pl.BlockSpec((tm, n), lambda i: (i, 0)),
        out_shape=jax.ShapeDtypeStruct((m, n), BF16),
        compiler_params=_cp("parallel"),
    )(oa, ob, h, w_oa, w_ob, w_ga, w_gb)


def _mixout_kernel(m_ref, x_ref, w_ref, gt_ref, g_ref, sh_ref, sc_ref, x1_ref, h2_ref, *, row_split):
    rs = m_ref.shape[0] // row_split
    for r in range(row_split):
        rows = slice(r * rs, (r + 1) * rs)
        x1 = x_ref[rows, :] + gt_ref[...] * _dot(m_ref[rows, :], w_ref[...])
        x1_ref[rows, :] = x1
        h2_ref[rows, :] = (_rms(x1, g_ref[...]) * (1.0 + sc_ref[...]) + sh_ref[...]).astype(BF16)


def _mixout(mm, x, w_out, g2, mod3, row0):
    bm, sm, d = x.shape
    ts = _tile(sm, 512)
    tok = pl.BlockSpec((None, ts, d), lambda b, i: (b, i, 0))
    return pl.pallas_call(
        functools.partial(_mixout_kernel, row_split=2 if ts % 16 == 0 else 1),
        grid=(bm, sm // ts),
        in_specs=[tok, tok, pl.BlockSpec((d, d), lambda b, i: (0, 0), pipeline_mode=pl.Buffered(1)),
                  _mod_spec(d, 2, row0, 2), pl.BlockSpec((1, d), lambda b, i: (0, 0)),
                  _mod_spec(d, 3, row0, 2), _mod_spec(d, 4, row0, 2)],
        out_specs=[tok, tok],
        out_shape=[jax.ShapeDtypeStruct((bm, sm, d), F32),
                   jax.ShapeDtypeStruct((bm, sm, d), BF16)],
        compiler_params=_cp("parallel", "parallel"),
    )(mm, x, w_out, mod3, g2.reshape(1, d), mod3, mod3)


def _ffn_up_kernel(h_ref, wg_ref, wu_ref, o_ref, *, row_split):
    rs = h_ref.shape[0] // row_split
    for r in range(row_split):
        h = h_ref[r * rs:(r + 1) * rs, :]
        g = _dot(h, wg_ref[...])
        u = _dot(h, wu_ref[...])
        o_ref[r * rs:(r + 1) * rs, :] = (g * _sigmoid(g) * u).astype(BF16)


def _ffn_up(h2, w_gu):
    m, d = h2.shape
    d_ff = w_gu.shape[1] // 2
    tm = _tile(m, 1024)
    tn = d_ff // 2 if d_ff % (2 * MXU_COLS) == 0 else _tile(d_ff, 512)
    nj = d_ff // tn
    wspec = lambda off: pl.BlockSpec((d, tn), lambda j, i: (0, j + off), pipeline_mode=pl.Buffered(1))
    return pl.pallas_call(
        functools.partial(_ffn_up_kernel, row_split=4 if tm % 32 == 0 else 1),
        grid=(nj, m // tm),
        in_specs=[pl.BlockSpec((tm, d), lambda j, i: (i, 0)), wspec(0), wspec(nj)],
        out_specs=pl.BlockSpec((tm, tn), lambda j, i: (i, j)),
        out_shape=jax.ShapeDtypeStruct((m, d_ff), BF16),
        compiler_params=_cp("arbitrary", "arbitrary"),
    )(h2, w_gu, w_gu)


def _ffn_down_kernel(a_ref, w_ref, x_hbm, gt_ref, g_ref, y_ref, x_buf, x_sem, *, row_split):
    b, i = pl.program_id(0), pl.program_id(1)
    ts = y_ref.shape[0]
    x_copy = pltpu.make_async_copy(x_hbm.at[b, pl.ds(i * ts, ts), :], x_buf, x_sem)
    x_copy.start()
    rs = ts // row_split
    d = y_ref.shape[1]
    tn = _tile(d, 512)
    for r in range(row_split):
        rows = slice(r * rs, (r + 1) * rs)
        for n in range(d // tn):
            cols = slice(n * tn, (n + 1) * tn)
            y_ref[rows, cols] = _dot(a_ref[rows, :], w_ref[:, cols])
        if r == 0:
            x_copy.wait()
        y_ref[rows, :] = _rms(x_buf[rows, :] + gt_ref[...] * y_ref[rows, :], g_ref[...])


def _ffn_down(hid, w_down, x1, norm_f_g, mod3, row0):
    bm, sm, d = x1.shape
    d_ff = w_down.shape[0]
    ts = _tile(sm, 512)
    return pl.pallas_call(
        functools.partial(_ffn_down_kernel, row_split=2 if ts % 16 == 0 else 1),
        grid=(bm, sm // ts),
        in_specs=[pl.BlockSpec((None, ts, d_ff), lambda b, i: (b, i, 0)),
                  pl.BlockSpec((d_ff, d), lambda b, i: (0, 0), pipeline_mode=pl.Buffered(1)),
                  pl.BlockSpec(memory_space=pl.ANY),
                  _mod_spec(d, 5, row0, 2),
                  pl.BlockSpec((1, d), lambda b, i: (0, 0))],
        out_specs=pl.BlockSpec((None, ts, d), lambda b, i: (b, i, 0)),
        out_shape=jax.ShapeDtypeStruct((bm, sm, d), F32),
        scratch_shapes=[pltpu.VMEM((ts, d), F32), pltpu.SemaphoreType.DMA(())],
        compiler_params=_cp("arbitrary", "arbitrary"),
    )(hid, w_down, x1, mod3, norm_f_g.reshape(1, d))


def _rope_tables(n_tokens):
    t = jnp.arange(n_tokens, dtype=jnp.int32)
    row = (t // GRID_W).astype(F32)
    col = (t % GRID_W).astype(F32)
    n_freq = ROPE_DIM // 4
    inv_freq = ROPE_THETA ** (-jnp.arange(n_freq, dtype=F32) / n_freq)
    ang = jnp.concatenate([row[:, None] * inv_freq, col[:, None] * inv_freq], axis=-1)
    cos, sin = jnp.cos(ang), jnp.sin(ang)
    z = jnp.zeros_like(cos)
    return (jnp.concatenate([cos, cos, z, z], axis=-1),
            jnp.concatenate([-sin, z, z, z], axis=-1),
            jnp.concatenate([z, sin, z, z], axis=-1))


def _layer(x, mod3, row0, w, rope_tabs, attend):
    bm, sm, d = x.shape
    m = bm * sm
    h, q_b, k_b, v_b = _qkv(x, w["norm1_g"], mod3, row0, w["w_q"], w["w_k"], w["w_v"],
                            attend.kv_dtype, attend.v_ones)
    h2d = h.reshape(m, d)
    q_a, ckv, kr128, k_a, v_a = _latent(h, w["w_lat"], w["q_norm_g"], w["kv_norm_g"], w["w_uq_p"],
                                        w["w_uk"], w["w_uv"], rope_tabs)
    o_a, o_b = attend(q_a, k_a, v_a, q_b, k_b, v_b)
    mm = _merge(o_a.reshape(m, -1), o_b.reshape(m, -1), h2d, w["w_oa"], w["w_ob"], w["w_ga"], w["w_gb"])
    x1, hn = _mixout(mm.reshape(bm, sm, d), x, w["w_out"], w["norm2_g"], mod3, row0)
    hid = _ffn_up(hn.reshape(m, d), w["w_gu"])
    x2 = _ffn_down(hid.reshape(bm, sm, -1), w["w_down"], x1, w["norm_f_g"], mod3, row0)
    return x2, ckv, kr128, k_b, v_b


class _PromptAttend:
    kv_dtype = F32
    v_ones = False

    def __init__(self, batch, seq):
        self.batch, self.seq = batch, seq

    def __call__(self, q_a, k_a, v_a, q_b, k_b, v_b):
        sh = lambda a: a.reshape(self.batch, self.seq, -1)
        o_a = _attention(sh(q_a), sh(k_a), sh(v_a), None, None, heads=H_A, dv=V_DIM, hb=H_A, tq=self.seq)
        o_b = _attention(sh(q_b), sh(k_b), sh(v_b), None, None, heads=H_B, dv=HD_B, hb=H_B, tq=self.seq)
        return o_a, o_b


class _SampleAttend:
    kv_dtype = BF16
    v_ones = True

    def __init__(self, k_ctx_a, v_ctx_a, k_ctx_b, v_ctx_b, bias):
        self.ctx = (k_ctx_a, v_ctx_a, k_ctx_b, v_ctx_b, bias)

    def __call__(self, q_a, k_a, v_a, q_b, k_b, v_b):
        k_ctx_a, v_ctx_a, k_ctx_b, v_ctx_b, bias = self.ctx
        o_a = _attention(q_a, k_a, v_a, k_ctx_a, v_ctx_a, heads=H_A, dv=V_DIM, hb=1, tq=1024, tk=512)
        o_b = _na_attention(q_b, k_b, v_b, k_ctx_b, v_ctx_b, bias)
        return o_a, o_b


def kernel(x_prompt, x_sample, cache_mla_ckv, cache_mla_krope, cache_na_k, cache_na_v, c, c_ctx,
           w_mod, b_mod, norm1_g, w_in, q_norm_g, kv_norm_g, w_uq, w_uk, w_uv, rpb,
           w_oa, w_ob, w_out, norm2_g, w_gu, w_down, norm_f_g):
    batch, seq, d = x_prompt.shape
    dec_batch, dec_seq, _ = x_sample.shape
    depth = w_mod.shape[0]
    assert depth == 1, "one trunk layer: the final norm is fused into the layer's last kernel"
    past = cache_mla_ckv.shape[2]
    q_lora, kv_lora = q_norm_g.shape[-1], kv_norm_g.shape[-1]
    na_w = H_B * HD_B
    rows = dec_seq // GRID_W
    assert rows % NA_QROWS == 0 and rows >= NA_KROWS

    n_cond = 1 + dec_batch
    r8 = -(-n_cond // 8) * 8
    cond = jnp.zeros((r8, d), F32).at[0].set(c_ctx).at[1:n_cond].set(c)

    l = 0
    mod3 = _adaln(cond, w_mod[l], b_mod[l]).reshape(r8, 1, 6 * d)

    wi = w_in[l]
    o = 0
    sec = {}
    for name, width in (("lat", q_lora + kv_lora + ROPE_DIM), ("q", na_w), ("k", na_w), ("v", na_w),
                        ("ga", d), ("gb", d)):
        sec[name] = wi[:, o:o + width]
        o += width
    lat_w = q_lora + kv_lora + LANES
    w_uq_p = jnp.pad(w_uq[l].reshape(q_lora, H_A, NOPE_DIM + ROPE_DIM),
                     ((0, 0), (0, 0), (0, QK_SLOT - NOPE_DIM - ROPE_DIM))).reshape(q_lora, H_A * QK_SLOT)
    w = {
        "w_lat": jnp.pad(sec["lat"], ((0, 0), (0, lat_w - sec["lat"].shape[1]))).astype(BF16),
        "w_q": sec["q"].astype(BF16), "w_k": sec["k"].astype(BF16), "w_v": sec["v"].astype(BF16),
        "w_ga": sec["ga"].astype(BF16), "w_gb": sec["gb"].astype(BF16),
        "w_uq_p": w_uq_p.astype(BF16), "w_uk": w_uk[l].astype(BF16), "w_uv": w_uv[l].astype(BF16),
        "w_oa": w_oa[l].astype(BF16), "w_ob": w_ob[l].astype(BF16), "w_out": w_out[l].astype(BF16),
        "w_gu": w_gu[l].astype(BF16), "w_down": w_down[l].astype(BF16),
        "norm1_g": norm1_g[l], "norm2_g": norm2_g[l], "q_norm_g": q_norm_g[l], "kv_norm_g": kv_norm_g[l],
        "norm_f_g": norm_f_g,
    }

    xp = x_prompt.reshape(1, batch * seq, d)
    yp, ckv_p, kr_p, k_b_p, v_b_p = _layer(xp, mod3, 0, w, None, _PromptAttend(batch, seq))

    kr_ctx = jnp.pad(cache_mla_krope[:, l], ((0, 0), (0, 0), (0, LANES - ROPE_DIM)))
    k_ctx_a, v_ctx_a = _kvexp(cache_mla_ckv[:, l], kr_ctx, w["w_uk"], w["w_uv"])
    attend = _SampleAttend(k_ctx_a, v_ctx_a,
                           cache_na_k[:, l].reshape(dec_batch, past, na_w),
                           cache_na_v[:, l].reshape(dec_batch, past, na_w),
                           _na_bias_tables(rpb[l], rows))
    ys, _, _, _, _ = _layer(x_sample, mod3, 1, w, _rope_tables(dec_seq), attend)

    return (yp.reshape(batch, seq, d), ys,
            ckv_p.reshape(batch, 1, seq, kv_lora),
            kr_p.reshape(batch, seq, LANES)[:, :, :ROPE_DIM].reshape(batch, 1, seq, ROPE_DIM),
            k_b_p.reshape(batch, 1, seq, H_B, HD_B),
            v_b_p.reshape(batch, 1, seq, H_B, HD_B))
```

```python
import functools

import jax
import jax.numpy as jnp
import numpy as np
from jax import lax
from jax.experimental import pallas as pl
from jax.experimental.pallas import tpu as pltpu

F32 = jnp.float32
BF16 = jnp.bfloat16

GRID_W = 64
H_A = 8
NOPE_DIM = 128
ROPE_DIM = 64
V_DIM = 128
H_B = 8
HD_B = 128
WIN_R_MAX = 8
WIN_C = 16
ROPE_THETA = 10000.0
NORM_EPS = 1e-6
NEG_INF = -1e30
LOG2E = 1.4426950408889634
MLA_QSCALE = (NOPE_DIM + ROPE_DIM) ** -0.5 * LOG2E
NA_QSCALE = HD_B ** -0.5 * LOG2E

LANES = 128
MXU_COLS = 256
QK_SLOT = 256
NA_QROWS = 4
NA_KROWS = NA_QROWS + WIN_R_MAX
VMEM_LIMIT = 56 * 2 ** 20

ROW_TILE = 512
FFN_UP_ROWS = 1024
COL_TILE = 512
ADALN_COLS = 1024
MLA_TQ, MLA_TK = 1024, 512
ATTN_TK = 1024


def _cp(*sem):
    return pltpu.CompilerParams(dimension_semantics=sem, vmem_limit_bytes=VMEM_LIMIT)


def _dot(a, b):
    return jnp.dot(a, b, preferred_element_type=F32)


def _dot_nt(a, b):
    return lax.dot_general(a, b, (((1,), (1,)), ((), ())), preferred_element_type=F32)


def _sigmoid(x):
    return 1.0 / (1.0 + jnp.exp(-x))


def _rms(x, g):
    return x * lax.rsqrt(jnp.mean(x * x, axis=-1, keepdims=True) + NORM_EPS) * g


def _tile(n, want):
    t = min(n, want)
    while n % t:
        t //= 2
    return t


def _adaln_kernel(c_ref, w_ref, b_ref, o_ref):
    c = c_ref[...]
    s = (c * _sigmoid(c)).astype(BF16)
    o_ref[...] = _dot(s, w_ref[...].astype(BF16)) + b_ref[...]


def _adaln(cond, w_mod, b_mod):
    r, d = cond.shape
    n = w_mod.shape[1]
    tn = _tile(n, ADALN_COLS)
    return pl.pallas_call(
        _adaln_kernel,
        grid=(n // tn,),
        in_specs=[pl.BlockSpec((r, d), lambda j: (0, 0)),
                  pl.BlockSpec((d, tn), lambda j: (0, j)),
                  pl.BlockSpec((1, tn), lambda j: (0, j))],
        out_specs=pl.BlockSpec((r, tn), lambda j: (0, j)),
        out_shape=jax.ShapeDtypeStruct((r, n), F32),
        compiler_params=_cp("arbitrary"),
    )(cond, w_mod, b_mod.reshape(1, n))


def _mod_spec(d, sec, row0, grid_rank):
    if grid_rank == 2:
        return pl.BlockSpec((None, 1, d), lambda b, i: (row0 + b, 0, sec))
    return pl.BlockSpec((None, 1, d), lambda b, i, j: (row0 + b, 0, sec))


def _qkv_kernel(x_ref, g_ref, sh_ref, sc_ref, wq_ref, wk_ref, wv_ref,
                h_ref, q_ref, k_ref, v_ref, *, row_split, cache_layout):
    rs = x_ref.shape[0] // row_split
    for r in range(row_split):
        rows = slice(r * rs, (r + 1) * rs)
        h = (_rms(x_ref[rows, :], g_ref[...]) * (1.0 + sc_ref[...]) + sh_ref[...]).astype(BF16)
        h_ref[rows, :] = h
        q_ref[rows, :] = (_dot(h, wq_ref[...]) * NA_QSCALE).astype(BF16)
        k = _dot(h, wk_ref[...])
        v = _dot(h, wv_ref[...])
        if cache_layout:
            for hh in range(H_B):
                head_rows = pl.ds(r * rs * H_B + hh, rs, stride=H_B)
                k_ref[head_rows, :] = k[:, hh * HD_B:(hh + 1) * HD_B]
                v_ref[head_rows, :] = v[:, hh * HD_B:(hh + 1) * HD_B]
        else:
            k_ref[rows, :] = k.astype(BF16)
            one = jnp.ones((rs, HD_B), BF16)
            for hh in range(H_B):
                v_ref[rows, 2 * hh * HD_B:(2 * hh + 1) * HD_B] = v[:, hh * HD_B:(hh + 1) * HD_B].astype(BF16)
                v_ref[rows, (2 * hh + 1) * HD_B:(2 * hh + 2) * HD_B] = one


def _qkv(x, g, mod3, row0, wq, wk, wv, cache_layout):
    bm, sm, d = x.shape
    n = wq.shape[1]
    ts = _tile(sm, ROW_TILE)
    wspec = pl.BlockSpec((d, n), lambda b, i: (0, 0), pipeline_mode=pl.Buffered(1))
    tok = lambda w: pl.BlockSpec((None, ts, w), lambda b, i: (b, i, 0))
    if cache_layout:
        kv_specs = [pl.BlockSpec((None, ts * H_B, HD_B), lambda b, i: (b, i, 0))] * 2
        kv_shapes = [jax.ShapeDtypeStruct((bm, sm * H_B, HD_B), F32)] * 2
    else:
        kv_specs = [tok(n), tok(2 * n)]
        kv_shapes = [jax.ShapeDtypeStruct((bm, sm, n), BF16), jax.ShapeDtypeStruct((bm, sm, 2 * n), BF16)]
    return pl.pallas_call(
        functools.partial(_qkv_kernel, row_split=2 if ts % 16 == 0 else 1, cache_layout=cache_layout),
        grid=(bm, sm // ts),
        in_specs=[tok(d), pl.BlockSpec((1, d), lambda b, i: (0, 0)),
                  _mod_spec(d, 0, row0, 2), _mod_spec(d, 1, row0, 2), wspec, wspec, wspec],
        out_specs=[tok(d), tok(n)] + kv_specs,
        out_shape=[jax.ShapeDtypeStruct((bm, sm, d), BF16), jax.ShapeDtypeStruct((bm, sm, n), BF16)] + kv_shapes,
        compiler_params=_cp("parallel", "parallel"),
    )(x, g.reshape(1, d), mod3, mod3, wq, wk, wv)


def _rope128(g, c, s1, s2):
    return g * c + pltpu.roll(g, 96, 1) * s1 + pltpu.roll(g, 32, 1) * s2


def _expand_kv(ckv, kr, wuk_ref, wuv_ref, k_ref, v_ref):
    c = ckv.astype(BF16)
    kn = _dot(c, wuk_ref[...])
    v = _dot(c, wuv_ref[...])
    krb = kr.astype(BF16)
    one = jnp.ones((c.shape[0], V_DIM), BF16)
    for hh in range(H_A):
        lo = hh * QK_SLOT
        k_ref[:, lo:lo + LANES] = kn[:, hh * NOPE_DIM:(hh + 1) * NOPE_DIM].astype(BF16)
        k_ref[:, lo + LANES:lo + QK_SLOT] = krb
        v_ref[:, 2 * hh * V_DIM:(2 * hh + 1) * V_DIM] = v[:, hh * V_DIM:(hh + 1) * V_DIM].astype(BF16)
        v_ref[:, (2 * hh + 1) * V_DIM:(2 * hh + 2) * V_DIM] = one


def _latent_kernel(*refs, q_lora, kv_lora, rope):
    if rope:
        (h_ref, wl_ref, qg_ref, kg_ref, wuq_ref, wuk_ref, wuv_ref, c_ref, s1_ref, s2_ref,
         q_ref, ckv_ref, kr_ref, k_ref, v_ref) = refs
    else:
        (h_ref, wl_ref, qg_ref, kg_ref, wuq_ref, wuk_ref, wuv_ref,
         q_ref, ckv_ref, kr_ref, k_ref, v_ref) = refs
    lat = _dot(h_ref[...], wl_ref[...])
    cq = lat[:, :q_lora]
    ckv = _rms(lat[:, q_lora:q_lora + kv_lora], kg_ref[...])
    kr = lat[:, q_lora + kv_lora:]
    ckv_ref[...] = ckv
    if rope:
        c, s1, s2 = c_ref[...], s1_ref[...], s2_ref[...]
        kr = _rope128(kr, c, s1, s2)
    kr_ref[...] = kr
    _expand_kv(ckv, kr, wuk_ref, wuv_ref, k_ref, v_ref)
    qa = _dot(_rms(cq, qg_ref[...]).astype(BF16), wuq_ref[...])
    for hh in range(H_A):
        lo = hh * QK_SLOT
        q_ref[:, lo:lo + LANES] = (qa[:, lo:lo + LANES] * MLA_QSCALE).astype(BF16)
        g = qa[:, lo + LANES:lo + QK_SLOT]
        if rope:
            g = _rope128(g, c, s1, s2)
        q_ref[:, lo + LANES:lo + QK_SLOT] = (g * MLA_QSCALE).astype(BF16)


def _latent(h, w_lat, q_norm_g, kv_norm_g, w_uq_p, w_uk, w_uv, rope_tabs):
    bm, sm, d = h.shape
    q_lora, kv_lora = q_norm_g.shape[-1], kv_norm_g.shape[-1]
    ts = _tile(sm, ROW_TILE)
    rope = rope_tabs is not None
    tok = lambda w: pl.BlockSpec((None, ts, w), lambda b, i: (b, i, 0))
    full = lambda a: pl.BlockSpec(a.shape, lambda b, i: (0, 0))
    qg, kg = q_norm_g.reshape(1, q_lora), kv_norm_g.reshape(1, kv_lora)
    args = [h, w_lat, qg, kg, w_uq_p, w_uk, w_uv]
    in_specs = [tok(d)] + [full(a) for a in args[1:]]
    if rope:
        in_specs += [pl.BlockSpec((ts, LANES), lambda b, i: (i, 0))] * 3
        args += list(rope_tabs)
    widths = (H_A * QK_SLOT, kv_lora, LANES, H_A * QK_SLOT, H_A * 2 * V_DIM)
    dtypes = (BF16, F32, F32, BF16, BF16)
    return pl.pallas_call(
        functools.partial(_latent_kernel, q_lora=q_lora, kv_lora=kv_lora, rope=rope),
        grid=(bm, sm // ts),
        in_specs=in_specs,
        out_specs=[tok(w) for w in widths],
        out_shape=[jax.ShapeDtypeStruct((bm, sm, w), t) for w, t in zip(widths, dtypes)],
        compiler_params=_cp("parallel", "parallel"),
    )(*args)


def _kvexp_kernel(ckv_ref, kr_ref, wuk_ref, wuv_ref, k_ref, v_ref):
    _expand_kv(ckv_ref[...], kr_ref[...], wuk_ref, wuv_ref, k_ref, v_ref)


def _kvexp(ckv, kr128, w_uk, w_uv):
    bm, sm, kv_lora = ckv.shape
    ts = _tile(sm, ROW_TILE)
    tok = lambda w: pl.BlockSpec((None, ts, w), lambda b, i: (b, i, 0))
    full = lambda a: pl.BlockSpec(a.shape, lambda b, i: (0, 0))
    widths = (H_A * QK_SLOT, H_A * 2 * V_DIM)
    return pl.pallas_call(
        _kvexp_kernel,
        grid=(bm, sm // ts),
        in_specs=[tok(kv_lora), tok(LANES), full(w_uk), full(w_uv)],
        out_specs=[tok(w) for w in widths],
        out_shape=[jax.ShapeDtypeStruct((bm, sm, w), BF16) for w in widths],
        compiler_params=_cp("parallel", "parallel"),
    )(ckv, kr128, w_uk, w_uv)


def _attn_kernel(*refs, hb, dk, dv, vw, two, tk, head_rows):
    ones = vw == 2 * dv
    if two:
        q_ref, k1_ref, v1_ref, k2_ref, v2_ref, o_ref = refs
    else:
        q_ref, k1_ref, v1_ref, o_ref = refs
    n_keys = k1_ref.shape[0] // max(head_rows, 1)
    chunks = [(k1_ref, v1_ref, c * tk, tk) for c in range(n_keys // tk)]
    if two:
        chunks.append((k2_ref, v2_ref, 0, k2_ref.shape[0]))

    def head_slab(ref, lo, n, j, w):
        if head_rows:
            return ref[pl.ds(lo * head_rows + j, n, stride=head_rows), :]
        return ref[lo:lo + n, j * w:(j + 1) * w]

    for j in range(hb):
        q = q_ref[:, j * dk:(j + 1) * dk]

        def logits(ch):
            k_ref, _, lo, n = ch
            return _dot_nt(q, head_slab(k_ref, lo, n, j, dk).astype(BF16))

        s_next = logits(chunks[0])
        m = l = acc = None
        for i, (_, v_ref, lo, n) in enumerate(chunks):
            s = s_next
            if i + 1 < len(chunks):
                s_next = logits(chunks[i + 1])
            mc = jnp.max(s, axis=-1, keepdims=True)
            m_new = mc if m is None else jnp.maximum(m, mc)
            p = jnp.exp2(s - m_new)
            pv = _dot(p.astype(BF16), head_slab(v_ref, lo, n, j, vw).astype(BF16))
            if not ones:
                ps = jnp.sum(p, axis=-1, keepdims=True)
            if m is None:
                acc = pv
                l = None if ones else ps
            else:
                alpha = jnp.exp2(m - m_new)
                acc = alpha * acc + pv
                l = None if ones else alpha * l + ps
            m = m_new
        if ones:
            o = acc[:, :dv] * (1.0 / acc[:, dv:])
        else:
            o = acc * (1.0 / l)
        o_ref[:, j * dv:(j + 1) * dv] = o.astype(o_ref.dtype)


def _attention(q, k1, v1, k2, v2, *, heads, dv, hb, tq, tk=ATTN_TK, head_rows=False):
    b, sq, qw = q.shape
    dk = qw // heads
    two = k2 is not None
    tq = _tile(sq, tq)
    if head_rows:
        assert hb == heads and not two
        vw, n_keys = v1.shape[-1], k1.shape[1] // heads
        kv = lambda a, w: pl.BlockSpec((None, a.shape[1], w), lambda bi, hg, qi: (bi, 0, 0))
    else:
        vw, n_keys = v1.shape[-1] // heads, k1.shape[1]
        kv = lambda a, w: pl.BlockSpec((None, a.shape[1], hb * w), lambda bi, hg, qi: (bi, 0, hg))
    in_specs = [pl.BlockSpec((None, tq, hb * dk), lambda bi, hg, qi: (bi, qi, hg)),
                kv(k1, dk), kv(v1, vw)]
    args = [q, k1, v1]
    if two:
        in_specs += [kv(k2, dk), kv(v2, vw)]
        args += [k2, v2]
    return pl.pallas_call(
        functools.partial(_attn_kernel, hb=hb, dk=dk, dv=dv, vw=vw, two=two,
                          tk=_tile(n_keys, tk), head_rows=heads if head_rows else 0),
        grid=(b, heads // hb, sq // tq),
        in_specs=in_specs,
        out_specs=pl.BlockSpec((None, tq, hb * dv), lambda bi, hg, qi: (bi, qi, hg)),
        out_shape=jax.ShapeDtypeStruct((b, sq, heads * dv), BF16),
        compiler_params=_cp("parallel", "parallel", "arbitrary"),
    )(*args)


def _na_bias_tables(rpb, rows):
    win_r = min(WIN_R_MAX, rows)
    qc = np.arange(GRID_W)[:, None]
    kc = np.arange(GRID_W)[None, :]
    qs = np.clip(qc - WIN_C // 2, 0, GRID_W - WIN_C)
    col_valid = (kc >= qs) & (kc < qs + WIN_C)
    col_off = np.clip(kc - qc + (WIN_C - 1), 0, 2 * WIN_C - 2)
    onehot = (col_off[None] == np.arange(2 * WIN_C - 1)[:, None, None]).astype(np.float32)
    e = jnp.einsum("hrc,cqk->hrqk", rpb, onehot, precision=lax.Precision.HIGHEST)
    e = jnp.where(col_valid, e * LOG2E, NEG_INF)
    n_roff = 2 * WIN_R_MAX - 1
    e = jnp.concatenate([e, jnp.full((H_B, 1, GRID_W, GRID_W), NEG_INF, F32)], axis=1)
    slab = []
    for r_blk in (0, NA_QROWS, rows - NA_QROWS):
        kr0 = int(np.clip(r_blk - win_r // 2, 0, rows - NA_KROWS))
        per_row = []
        for a in range(NA_QROWS):
            r = r_blk + a
            r0 = int(np.clip(r - win_r // 2, 0, rows - win_r))
            per_row.append([kr - r + (WIN_R_MAX - 1) if r0 <= kr < r0 + win_r else n_roff
                            for kr in range(kr0, kr0 + NA_KROWS)])
        slab.append(per_row)

    def assemble(e_ref, o_ref):
        for v, per_row in enumerate(slab):
            for a, idx in enumerate(per_row):
                o_ref[v, a * GRID_W:(a + 1) * GRID_W, :] = jnp.concatenate([e_ref[j] for j in idx], axis=-1)

    return pl.pallas_call(
        assemble,
        grid=(H_B,),
        in_specs=[pl.BlockSpec((None, n_roff + 1, GRID_W, GRID_W), lambda h: (h, 0, 0, 0))],
        out_specs=pl.BlockSpec((None, 3, NA_QROWS * GRID_W, NA_KROWS * GRID_W), lambda h: (h, 0, 0, 0)),
        out_shape=jax.ShapeDtypeStruct((H_B, 3, NA_QROWS * GRID_W, NA_KROWS * GRID_W), F32),
        compiler_params=_cp("parallel"),
    )(e)


def _na_kernel(q_ref, k_ref, v_ref, kc_ref, vc_ref, bias_ref, o_ref, *, rows):
    nblk = rows // NA_QROWS
    nq, nk = NA_QROWS * GRID_W, NA_KROWS * GRID_W
    n_ctx = kc_ref.shape[0] // H_B
    head_rows = pl.ds(pl.program_id(1), n_ctx, stride=H_B)
    kc = kc_ref[head_rows, :].astype(BF16)
    vc = jnp.concatenate([vc_ref[head_rows, :].astype(BF16), jnp.ones((n_ctx, HD_B), BF16)], axis=-1)

    for t in range(nblk):
        r_blk = t * NA_QROWS
        kr0 = int(np.clip(r_blk - min(WIN_R_MAX, rows) // 2, 0, rows - NA_KROWS))
        var = 0 if t == 0 else (2 if t == nblk - 1 else 1)
        q0, k0 = r_blk * GRID_W, kr0 * GRID_W
        q = q_ref[q0:q0 + nq, :]
        s = _dot_nt(q, k_ref[k0:k0 + nk, :]) + bias_ref[var]
        sc = _dot_nt(q, kc)
        m = jnp.maximum(jnp.max(s, axis=-1, keepdims=True), jnp.max(sc, axis=-1, keepdims=True))
        p = jnp.exp2(s - m)
        pc = jnp.exp2(sc - m)
        o = _dot(p.astype(BF16), v_ref[k0:k0 + nk, :]) + _dot(pc.astype(BF16), vc)
        o_ref[q0:q0 + nq, :] = (o[:, :HD_B] * (1.0 / o[:, HD_B:])).astype(o_ref.dtype)


def _na_attention(q, k, v, k_ctx, v_ctx, bias):
    b, s, _ = q.shape
    rows = s // GRID_W
    head = lambda n: pl.BlockSpec((None, n, HD_B), lambda bi, h: (bi, 0, h))
    ctx = pl.BlockSpec((None,) + k_ctx.shape[1:], lambda bi, h: (bi, 0, 0))
    return pl.pallas_call(
        functools.partial(_na_kernel, rows=rows),
        grid=(b, H_B),
        in_specs=[head(s), head(s),
                  pl.BlockSpec((None, s, 2 * HD_B), lambda bi, h: (bi, 0, h)),
                  ctx, ctx,
                  pl.BlockSpec((None,) + bias.shape[1:], lambda bi, h: (h, 0, 0, 0))],
        out_specs=head(s),
        out_shape=jax.ShapeDtypeStruct((b, s, H_B * HD_B), BF16),
        compiler_params=_cp("parallel", "arbitrary"),
    )(q, k, v, k_ctx, v_ctx, bias)


def _merge_kernel(oa_ref, ob_ref, h_ref, woa_ref, wob_ref, wga_ref, wgb_ref, m_ref, *, row_split):
    rs = h_ref.shape[0] // row_split
    for r in range(row_split):
        rows = slice(r * rs, (r + 1) * rs)
        h = h_ref[rows, :]
        ya = _dot(oa_ref[rows, :], woa_ref[...])
        yb = _dot(ob_ref[rows, :], wob_ref[...])
        ga = _sigmoid(_dot(h, wga_ref[...]))
        gb = _sigmoid(_dot(h, wgb_ref[...]))
        m_ref[rows, :] = (ga * ya + gb * yb).astype(BF16)


def _merge(oa, ob, h, w_oa, w_ob, w_ga, w_gb):
    m, d = h.shape
    n = w_oa.shape[1]
    tm = _tile(m, ROW_TILE)
    row = lambda a: pl.BlockSpec((tm, a.shape[1]), lambda i: (i, 0))
    full = lambda a: pl.BlockSpec(a.shape, lambda i: (0, 0), pipeline_mode=pl.Buffered(1))
    return pl.pallas_call(
        functools.partial(_merge_kernel, row_split=2 if tm % 16 == 0 else 1),
        grid=(m // tm,),
        in_specs=[row(oa), row(ob), row(h), full(w_oa), full(w_ob), full(w_ga), full(w_gb)],
        out_specs=pl.BlockSpec((tm, n), lambda i: (i, 0)),
        out_shape=jax.ShapeDtypeStruct((m, n), BF16),
        compiler_params=_cp("parallel"),
    )(oa, ob, h, w_oa, w_ob, w_ga, w_gb)


def _mixout_kernel(m_ref, x_ref, w_ref, gt_ref, g_ref, sh_ref, sc_ref, x1_ref, h2_ref, *, row_split):
    rs = m_ref.shape[0] // row_split
    for r in range(row_split):
        rows = slice(r * rs, (r + 1) * rs)
        x1 = x_ref[rows, :] + gt_ref[...] * _dot(m_ref[rows, :], w_ref[...])
        x1_ref[rows, :] = x1
        h2_ref[rows, :] = (_rms(x1, g_ref[...]) * (1.0 + sc_ref[...]) + sh_ref[...]).astype(BF16)


def _mixout(mm, x, w_out, g2, mod3, row0):
    bm, sm, d = x.shape
    ts = _tile(sm, ROW_TILE)
    tok = pl.BlockSpec((None, ts, d), lambda b, i: (b, i, 0))
    return pl.pallas_call(
        functools.partial(_mixout_kernel, row_split=2 if ts % 16 == 0 else 1),
        grid=(bm, sm // ts),
        in_specs=[tok, tok, pl.BlockSpec((d, d), lambda b, i: (0, 0), pipeline_mode=pl.Buffered(1)),
                  _mod_spec(d, 2, row0, 2), pl.BlockSpec((1, d), lambda b, i: (0, 0)),
                  _mod_spec(d, 3, row0, 2), _mod_spec(d, 4, row0, 2)],
        out_specs=[tok, tok],
        out_shape=[jax.ShapeDtypeStruct((bm, sm, d), F32),
                   jax.ShapeDtypeStruct((bm, sm, d), BF16)],
        compiler_params=_cp("parallel", "parallel"),
    )(mm, x, w_out, mod3, g2.reshape(1, d), mod3, mod3)


def _ffn_up_kernel(h_ref, wg_ref, wu_ref, o_ref, *, row_split):
    rs = h_ref.shape[0] // row_split
    for r in range(row_split):
        h = h_ref[r * rs:(r + 1) * rs, :]
        g = _dot(h, wg_ref[...])
        u = _dot(h, wu_ref[...])
        o_ref[r * rs:(r + 1) * rs, :] = (g * _sigmoid(g) * u).astype(BF16)


def _ffn_up(h2, w_gu):
    m, d = h2.shape
    d_ff = w_gu.shape[1] // 2
    tm = _tile(m, FFN_UP_ROWS)
    tn = d_ff // 2 if d_ff % (2 * MXU_COLS) == 0 else _tile(d_ff, COL_TILE)
    nj = d_ff // tn
    wspec = lambda off: pl.BlockSpec((d, tn), lambda j, i: (0, j + off), pipeline_mode=pl.Buffered(1))
    return pl.pallas_call(
        functools.partial(_ffn_up_kernel, row_split=4 if tm % 32 == 0 else 1),
        grid=(nj, m // tm),
        in_specs=[pl.BlockSpec((tm, d), lambda j, i: (i, 0)), wspec(0), wspec(nj)],
        out_specs=pl.BlockSpec((tm, tn), lambda j, i: (i, j)),
        out_shape=jax.ShapeDtypeStruct((m, d_ff), BF16),
        compiler_params=_cp("arbitrary", "arbitrary"),
    )(h2, w_gu, w_gu)


def _ffn_down_kernel(a_ref, w_ref, x_hbm, gt_ref, g_ref, y_ref, x_buf, x_sem, *, row_split):
    b, i = pl.program_id(0), pl.program_id(1)
    ts = y_ref.shape[0]
    x_copy = pltpu.make_async_copy(x_hbm.at[b, pl.ds(i * ts, ts), :], x_buf, x_sem)
    x_copy.start()
    rs = ts // row_split
    d = y_ref.shape[1]
    tn = _tile(d, COL_TILE)
    for r in range(row_split):
        rows = slice(r * rs, (r + 1) * rs)
        for n in range(d // tn):
            cols = slice(n * tn, (n + 1) * tn)
            y_ref[rows, cols] = _dot(a_ref[rows, :], w_ref[:, cols])
        if r == 0:
            x_copy.wait()
        y_ref[rows, :] = _rms(x_buf[rows, :] + gt_ref[...] * y_ref[rows, :], g_ref[...])


def _ffn_down(hid, w_down, x1, norm_f_g, mod3, row0):
    bm, sm, d = x1.shape
    d_ff = w_down.shape[0]
    ts = _tile(sm, ROW_TILE)
    return pl.pallas_call(
        functools.partial(_ffn_down_kernel, row_split=2 if ts % 16 == 0 else 1),
        grid=(bm, sm // ts),
        in_specs=[pl.BlockSpec((None, ts, d_ff), lambda b, i: (b, i, 0)),
                  pl.BlockSpec((d_ff, d), lambda b, i: (0, 0), pipeline_mode=pl.Buffered(1)),
                  pl.BlockSpec(memory_space=pl.ANY),
                  _mod_spec(d, 5, row0, 2),
                  pl.BlockSpec((1, d), lambda b, i: (0, 0))],
        out_specs=pl.BlockSpec((None, ts, d), lambda b, i: (b, i, 0)),
        out_shape=jax.ShapeDtypeStruct((bm, sm, d), F32),
        scratch_shapes=[pltpu.VMEM((ts, d), F32), pltpu.SemaphoreType.DMA(())],
        compiler_params=_cp("arbitrary", "arbitrary"),
    )(hid, w_down, x1, mod3, norm_f_g.reshape(1, d))


def _rope_tables(n_tokens):
    t = jnp.arange(n_tokens, dtype=jnp.int32)
    row = (t // GRID_W).astype(F32)
    col = (t % GRID_W).astype(F32)
    n_freq = ROPE_DIM // 4
    inv_freq = ROPE_THETA ** (-jnp.arange(n_freq, dtype=F32) / n_freq)
    ang = jnp.concatenate([row[:, None] * inv_freq, col[:, None] * inv_freq], axis=-1)
    cos, sin = jnp.cos(ang), jnp.sin(ang)
    z = jnp.zeros_like(cos)
    return (jnp.concatenate([cos, cos, z, z], axis=-1),
            jnp.concatenate([-sin, z, z, z], axis=-1),
            jnp.concatenate([z, sin, z, z], axis=-1))


def _layer(x, mod3, row0, w, rope_tabs, attend):
    bm, sm, d = x.shape
    m = bm * sm
    h, q_b, k_b, v_b = _qkv(x, w["norm1_g"], mod3, row0, w["w_q"], w["w_k"], w["w_v"], attend.cache_layout)
    h2d = h.reshape(m, d)
    q_a, ckv, kr128, k_a, v_a = _latent(h, w["w_lat"], w["q_norm_g"], w["kv_norm_g"], w["w_uq_p"],
                                        w["w_uk"], w["w_uv"], rope_tabs)
    o_a, o_b = attend(q_a, k_a, v_a, q_b, k_b, v_b)
    mm = _merge(o_a.reshape(m, -1), o_b.reshape(m, -1), h2d, w["w_oa"], w["w_ob"], w["w_ga"], w["w_gb"])
    x1, hn = _mixout(mm.reshape(bm, sm, d), x, w["w_out"], w["norm2_g"], mod3, row0)
    hid = _ffn_up(hn.reshape(m, d), w["w_gu"])
    x2 = _ffn_down(hid.reshape(bm, sm, -1), w["w_down"], x1, w["norm_f_g"], mod3, row0)
    return x2, ckv, kr128, k_b, v_b


class _PromptAttend:
    cache_layout = True

    def __init__(self, batch, seq):
        self.batch, self.seq = batch, seq

    def __call__(self, q_a, k_a, v_a, q_b, k_b, v_b):
        sh = lambda a: a.reshape(self.batch, self.seq, -1)
        o_a = _attention(sh(q_a), sh(k_a), sh(v_a), None, None, heads=H_A, dv=V_DIM, hb=H_A, tq=self.seq)
        rows = lambda a: a.reshape(self.batch, self.seq * H_B, HD_B)
        o_b = _attention(sh(q_b), rows(k_b), rows(v_b), None, None, heads=H_B, dv=HD_B, hb=H_B,
                         tq=self.seq, head_rows=True)
        return o_a, o_b


class _SampleAttend:
    cache_layout = False

    def __init__(self, k_ctx_a, v_ctx_a, k_ctx_b, v_ctx_b, bias):
        self.ctx = (k_ctx_a, v_ctx_a, k_ctx_b, v_ctx_b, bias)

    def __call__(self, q_a, k_a, v_a, q_b, k_b, v_b):
        k_ctx_a, v_ctx_a, k_ctx_b, v_ctx_b, bias = self.ctx
        o_a = _attention(q_a, k_a, v_a, k_ctx_a, v_ctx_a, heads=H_A, dv=V_DIM, hb=1, tq=MLA_TQ, tk=MLA_TK)
        o_b = _na_attention(q_b, k_b, v_b, k_ctx_b, v_ctx_b, bias)
        return o_a, o_b


def kernel(x_prompt, x_sample, cache_mla_ckv, cache_mla_krope, cache_na_k, cache_na_v, c, c_ctx,
           w_mod, b_mod, norm1_g, w_in, q_norm_g, kv_norm_g, w_uq, w_uk, w_uv, rpb,
           w_oa, w_ob, w_out, norm2_g, w_gu, w_down, norm_f_g):
    batch, seq, d = x_prompt.shape
    dec_batch, dec_seq, _ = x_sample.shape
    depth = w_mod.shape[0]
    assert depth == 1, "one trunk layer: the final norm is fused into the layer's last kernel"
    past = cache_mla_ckv.shape[2]
    q_lora, kv_lora = q_norm_g.shape[-1], kv_norm_g.shape[-1]
    na_w = H_B * HD_B
    rows = dec_seq // GRID_W
    assert rows % NA_QROWS == 0 and rows >= NA_KROWS

    n_cond = 1 + dec_batch
    r8 = -(-n_cond // 8) * 8
    cond = jnp.zeros((r8, d), F32).at[0].set(c_ctx).at[1:n_cond].set(c)

    l = 0
    mod3 = _adaln(cond, w_mod[l], b_mod[l]).reshape(r8, 1, 6 * d)

    wi = w_in[l]
    o = 0
    sec = {}
    for name, width in (("lat", q_lora + kv_lora + ROPE_DIM), ("q", na_w), ("k", na_w), ("v", na_w),
                        ("ga", d), ("gb", d)):
        sec[name] = wi[:, o:o + width]
        o += width
    lat_w = q_lora + kv_lora + LANES
    w_uq_p = jnp.pad(w_uq[l].reshape(q_lora, H_A, NOPE_DIM + ROPE_DIM),
                     ((0, 0), (0, 0), (0, QK_SLOT - NOPE_DIM - ROPE_DIM))).reshape(q_lora, H_A * QK_SLOT)
    w = {
        "w_lat": jnp.pad(sec["lat"], ((0, 0), (0, lat_w - sec["lat"].shape[1]))).astype(BF16),
        "w_q": sec["q"].astype(BF16), "w_k": sec["k"].astype(BF16), "w_v": sec["v"].astype(BF16),
        "w_ga": sec["ga"].astype(BF16), "w_gb": sec["gb"].astype(BF16),
        "w_uq_p": w_uq_p.astype(BF16), "w_uk": w_uk[l].astype(BF16), "w_uv": w_uv[l].astype(BF16),
        "w_oa": w_oa[l].astype(BF16), "w_ob": w_ob[l].astype(BF16), "w_out": w_out[l].astype(BF16),
        "w_gu": w_gu[l].astype(BF16), "w_down": w_down[l].astype(BF16),
        "norm1_g": norm1_g[l], "norm2_g": norm2_g[l], "q_norm_g": q_norm_g[l], "kv_norm_g": kv_norm_g[l],
        "norm_f_g": norm_f_g,
    }

    xp = x_prompt.reshape(1, batch * seq, d)
    yp, ckv_p, kr_p, k_b_p, v_b_p = _layer(xp, mod3, 0, w, None, _PromptAttend(batch, seq))

    kr_ctx = jnp.pad(cache_mla_krope[:, l], ((0, 0), (0, 0), (0, LANES - ROPE_DIM)))
    k_ctx_a, v_ctx_a = _kvexp(cache_mla_ckv[:, l], kr_ctx, w["w_uk"], w["w_uv"])
    attend = _SampleAttend(k_ctx_a, v_ctx_a,
                           cache_na_k[:, l].reshape(dec_batch, past * H_B, HD_B),
                           cache_na_v[:, l].reshape(dec_batch, past * H_B, HD_B),
                           _na_bias_tables(rpb[l], rows))
    ys, _, _, _, _ = _layer(x_sample, mod3, 1, w, _rope_tables(dec_seq), attend)

    return (yp.reshape(batch, seq, d), ys,
            ckv_p.reshape(batch, 1, seq, kv_lora),
            kr_p.reshape(batch, seq, LANES)[:, :, :ROPE_DIM].reshape(batch, 1, seq, ROPE_DIM),
            k_b_p.reshape(batch, 1, seq, H_B, HD_B),
            v_b_p.reshape(batch, 1, seq, H_B, HD_B))
```

```python
import functools

import jax
import jax.numpy as jnp
import numpy as np
from jax import lax
from jax.experimental import pallas as pl
from jax.experimental.pallas import tpu as pltpu

F32 = jnp.float32
BF16 = jnp.bfloat16

GRID_W = 64
H_A = 8
NOPE_DIM = 128
ROPE_DIM = 64
V_DIM = 128
H_B = 8
HD_B = 128
WIN_R_MAX = 8
WIN_C = 16
ROPE_THETA = 10000.0
NORM_EPS = 1e-6
NEG_INF = -1e30
LOG2E = 1.4426950408889634
MLA_QSCALE = (NOPE_DIM + ROPE_DIM) ** -0.5 * LOG2E
NA_QSCALE = HD_B ** -0.5 * LOG2E

LANES = 128
MXU_COLS = 256
QK_SLOT = 256
NA_QROWS = 4
NA_KROWS = NA_QROWS + WIN_R_MAX
VMEM_LIMIT = 56 * 2 ** 20

ROW_TILE = 512
FFN_UP_ROWS = 1024
COL_TILE = 512
ADALN_COLS = 1024
MLA_TQ, MLA_TK = 1024, 512
ATTN_TK = 1024


def _cp(*sem):
    return pltpu.CompilerParams(dimension_semantics=sem, vmem_limit_bytes=VMEM_LIMIT)


def _dot(a, b):
    return jnp.dot(a, b, preferred_element_type=F32)


def _dot_nt(a, b):
    return lax.dot_general(a, b, (((1,), (1,)), ((), ())), preferred_element_type=F32)


def _sigmoid(x):
    return 1.0 / (1.0 + jnp.exp(-x))


def _rms(x, g):
    return x * lax.rsqrt(jnp.mean(x * x, axis=-1, keepdims=True) + NORM_EPS) * g


def _tile(n, want):
    t = min(n, want)
    while n % t:
        t //= 2
    return t


def _adaln_kernel(c_ref, w_ref, b_ref, o_ref):
    c = c_ref[...]
    s = (c * _sigmoid(c)).astype(BF16)
    o_ref[...] = _dot(s, w_ref[...].astype(BF16)) + b_ref[...]


def _adaln(cond, w_mod, b_mod):
    r, d = cond.shape
    n = w_mod.shape[1]
    tn = _tile(n, ADALN_COLS)
    return pl.pallas_call(
        _adaln_kernel,
        grid=(n // tn,),
        in_specs=[pl.BlockSpec((r, d), lambda j: (0, 0)),
                  pl.BlockSpec((d, tn), lambda j: (0, j)),
                  pl.BlockSpec((1, tn), lambda j: (0, j))],
        out_specs=pl.BlockSpec((r, tn), lambda j: (0, j)),
        out_shape=jax.ShapeDtypeStruct((r, n), F32),
        compiler_params=_cp("arbitrary"),
    )(cond, w_mod, b_mod.reshape(1, n))


def _mod_spec(d, sec, row0, grid_rank):
    if grid_rank == 2:
        return pl.BlockSpec((None, 1, d), lambda b, i: (row0 + b, 0, sec))
    return pl.BlockSpec((None, 1, d), lambda b, i, j: (row0 + b, 0, sec))


def _qkv_kernel(x_ref, g_ref, sh_ref, sc_ref, wq_ref, wk_ref, wv_ref,
                h_ref, q_ref, k_ref, v_ref, *, row_split, cache_layout):
    rs = x_ref.shape[0] // row_split
    for r in range(row_split):
        rows = slice(r * rs, (r + 1) * rs)
        h = (_rms(x_ref[rows, :], g_ref[...]) * (1.0 + sc_ref[...]) + sh_ref[...]).astype(BF16)
        h_ref[rows, :] = h
        q_ref[rows, :] = (_dot(h, wq_ref[...]) * NA_QSCALE).astype(BF16)
        k = _dot(h, wk_ref[...])
        v = _dot(h, wv_ref[...])
        if cache_layout:
            for hh in range(H_B):
                head_rows = pl.ds(r * rs * H_B + hh, rs, stride=H_B)
                k_ref[head_rows, :] = k[:, hh * HD_B:(hh + 1) * HD_B]
                v_ref[head_rows, :] = v[:, hh * HD_B:(hh + 1) * HD_B]
        else:
            k_ref[rows, :] = k.astype(BF16)
            one = jnp.ones((rs, HD_B), BF16)
            for hh in range(H_B):
                v_ref[rows, 2 * hh * HD_B:(2 * hh + 1) * HD_B] = v[:, hh * HD_B:(hh + 1) * HD_B].astype(BF16)
                v_ref[rows, (2 * hh + 1) * HD_B:(2 * hh + 2) * HD_B] = one


def _qkv(x, g, mod3, row0, wq, wk, wv, cache_layout):
    bm, sm, d = x.shape
    n = wq.shape[1]
    ts = _tile(sm, ROW_TILE)
    wspec = pl.BlockSpec((d, n), lambda b, i: (0, 0), pipeline_mode=pl.Buffered(1))
    tok = lambda w: pl.BlockSpec((None, ts, w), lambda b, i: (b, i, 0))
    if cache_layout:
        kv_specs = [pl.BlockSpec((None, ts * H_B, HD_B), lambda b, i: (b, i, 0))] * 2
        kv_shapes = [jax.ShapeDtypeStruct((bm, sm * H_B, HD_B), F32)] * 2
    else:
        kv_specs = [tok(n), tok(2 * n)]
        kv_shapes = [jax.ShapeDtypeStruct((bm, sm, n), BF16), jax.ShapeDtypeStruct((bm, sm, 2 * n), BF16)]
    return pl.pallas_call(
        functools.partial(_qkv_kernel, row_split=2 if ts % 16 == 0 else 1, cache_layout=cache_layout),
        grid=(bm, sm // ts),
        in_specs=[tok(d), pl.BlockSpec((1, d), lambda b, i: (0, 0)),
                  _mod_spec(d, 0, row0, 2), _mod_spec(d, 1, row0, 2), wspec, wspec, wspec],
        out_specs=[tok(d), tok(n)] + kv_specs,
        out_shape=[jax.ShapeDtypeStruct((bm, sm, d), BF16), jax.ShapeDtypeStruct((bm, sm, n), BF16)] + kv_shapes,
        compiler_params=_cp("parallel", "parallel"),
    )(x, g.reshape(1, d), mod3, mod3, wq, wk, wv)


def _rope128(g, c, s1, s2):
    return g * c + pltpu.roll(g, 96, 1) * s1 + pltpu.roll(g, 32, 1) * s2


def _expand_kv(ckv, kr, wuk_ref, wuv_ref, k_ref, v_ref):
    c = ckv.astype(BF16)
    kn = _dot(c, wuk_ref[...])
    v = _dot(c, wuv_ref[...])
    krb = kr.astype(BF16)
    one = jnp.ones((c.shape[0], V_DIM), BF16)
    for hh in range(H_A):
        lo = hh * QK_SLOT
        k_ref[:, lo:lo + LANES] = kn[:, hh * NOPE_DIM:(hh + 1) * NOPE_DIM].astype(BF16)
        k_ref[:, lo + LANES:lo + QK_SLOT] = krb
        v_ref[:, 2 * hh * V_DIM:(2 * hh + 1) * V_DIM] = v[:, hh * V_DIM:(hh + 1) * V_DIM].astype(BF16)
        v_ref[:, (2 * hh + 1) * V_DIM:(2 * hh + 2) * V_DIM] = one


def _latent_kernel(*refs, q_lora, kv_lora, rope):
    if rope:
        (h_ref, wl_ref, qg_ref, kg_ref, wuq_ref, wuk_ref, wuv_ref, c_ref, s1_ref, s2_ref,
         q_ref, ckv_ref, kr_ref, k_ref, v_ref) = refs
    else:
        (h_ref, wl_ref, qg_ref, kg_ref, wuq_ref, wuk_ref, wuv_ref,
         q_ref, ckv_ref, kr_ref, k_ref, v_ref) = refs
    lat = _dot(h_ref[...], wl_ref[...])
    cq = lat[:, :q_lora]
    ckv = _rms(lat[:, q_lora:q_lora + kv_lora], kg_ref[...])
    kr = lat[:, q_lora + kv_lora:]
    ckv_ref[...] = ckv
    if rope:
        c, s1, s2 = c_ref[...], s1_ref[...], s2_ref[...]
        kr = _rope128(kr, c, s1, s2)
    kr_ref[...] = kr
    _expand_kv(ckv, kr, wuk_ref, wuv_ref, k_ref, v_ref)
    qa = _dot(_rms(cq, qg_ref[...]).astype(BF16), wuq_ref[...])
    for hh in range(H_A):
        lo = hh * QK_SLOT
        q_ref[:, lo:lo + LANES] = (qa[:, lo:lo + LANES] * MLA_QSCALE).astype(BF16)
        g = qa[:, lo + LANES:lo + QK_SLOT]
        if rope:
            g = _rope128(g, c, s1, s2)
        q_ref[:, lo + LANES:lo + QK_SLOT] = (g * MLA_QSCALE).astype(BF16)


def _latent(h, w_lat, q_norm_g, kv_norm_g, w_uq_p, w_uk, w_uv, rope_tabs):
    bm, sm, d = h.shape
    q_lora, kv_lora = q_norm_g.shape[-1], kv_norm_g.shape[-1]
    ts = _tile(sm, ROW_TILE)
    rope = rope_tabs is not None
    tok = lambda w: pl.BlockSpec((None, ts, w), lambda b, i: (b, i, 0))
    full = lambda a: pl.BlockSpec(a.shape, lambda b, i: (0, 0))
    qg, kg = q_norm_g.reshape(1, q_lora), kv_norm_g.reshape(1, kv_lora)
    args = [h, w_lat, qg, kg, w_uq_p, w_uk, w_uv]
    in_specs = [tok(d)] + [full(a) for a in args[1:]]
    if rope:
        in_specs += [pl.BlockSpec((ts, LANES), lambda b, i: (i, 0))] * 3
        args += list(rope_tabs)
    widths = (H_A * QK_SLOT, kv_lora, LANES, H_A * QK_SLOT, H_A * 2 * V_DIM)
    dtypes = (BF16, F32, F32, BF16, BF16)
    return pl.pallas_call(
        functools.partial(_latent_kernel, q_lora=q_lora, kv_lora=kv_lora, rope=rope),
        grid=(bm, sm // ts),
        in_specs=in_specs,
        out_specs=[tok(w) for w in widths],
        out_shape=[jax.ShapeDtypeStruct((bm, sm, w), t) for w, t in zip(widths, dtypes)],
        compiler_params=_cp("parallel", "parallel"),
    )(*args)


def _kvexp_kernel(ckv_ref, kr_ref, wuk_ref, wuv_ref, k_ref, v_ref):
    _expand_kv(ckv_ref[...], kr_ref[...], wuk_ref, wuv_ref, k_ref, v_ref)


def _kvexp(ckv, kr128, w_uk, w_uv):
    bm, sm, kv_lora = ckv.shape
    ts = _tile(sm, ROW_TILE)
    tok = lambda w: pl.BlockSpec((None, ts, w), lambda b, i: (b, i, 0))
    full = lambda a: pl.BlockSpec(a.shape, lambda b, i: (0, 0))
    widths = (H_A * QK_SLOT, H_A * 2 * V_DIM)
    return pl.pallas_call(
        _kvexp_kernel,
        grid=(bm, sm // ts),
        in_specs=[tok(kv_lora), tok(LANES), full(w_uk), full(w_uv)],
        out_specs=[tok(w) for w in widths],
        out_shape=[jax.ShapeDtypeStruct((bm, sm, w), BF16) for w in widths],
        compiler_params=_cp("parallel", "parallel"),
    )(ckv, kr128, w_uk, w_uv)


def _attn_kernel(*refs, hb, dk, dv, vw, two, tk, head_rows):
    ones = vw == 2 * dv
    if two:
        q_ref, k1_ref, v1_ref, k2_ref, v2_ref, o_ref = refs
    else:
        q_ref, k1_ref, v1_ref, o_ref = refs
    n_keys = k1_ref.shape[0] // max(head_rows, 1)
    chunks = [(k1_ref, v1_ref, c * tk, tk) for c in range(n_keys // tk)]
    if two:
        chunks.append((k2_ref, v2_ref, 0, k2_ref.shape[0]))

    def head_slab(ref, lo, n, j, w):
        if head_rows:
            return ref[pl.ds(lo * head_rows + j, n, stride=head_rows), :]
        return ref[lo:lo + n, j * w:(j + 1) * w]

    for j in range(hb):
        q = q_ref[:, j * dk:(j + 1) * dk]

        def logits(ch):
            k_ref, _, lo, n = ch
            return _dot_nt(q, head_slab(k_ref, lo, n, j, dk).astype(BF16))

        s_next = logits(chunks[0])
        m = l = acc = None
        for i, (_, v_ref, lo, n) in enumerate(chunks):
            s = s_next
            if i + 1 < len(chunks):
                s_next = logits(chunks[i + 1])
            mc = jnp.max(s, axis=-1, keepdims=True)
            m_new = mc if m is None else jnp.maximum(m, mc)
            p = jnp.exp2(s - m_new)
            pv = _dot(p.astype(BF16), head_slab(v_ref, lo, n, j, vw).astype(BF16))
            if not ones:
                ps = jnp.sum(p, axis=-1, keepdims=True)
            if m is None:
                acc = pv
                l = None if ones else ps
            else:
                alpha = jnp.exp2(m - m_new)
                acc = alpha * acc + pv
                l = None if ones else alpha * l + ps
            m = m_new
        if ones:
            o = acc[:, :dv] * (1.0 / acc[:, dv:])
        else:
            o = acc * (1.0 / l)
        o_ref[:, j * dv:(j + 1) * dv] = o.astype(o_ref.dtype)


def _attention(q, k1, v1, k2, v2, *, heads, dv, hb, tq, tk=ATTN_TK, head_rows=False):
    b, sq, qw = q.shape
    dk = qw // heads
    two = k2 is not None
    tq = _tile(sq, tq)
    if head_rows:
        assert hb == heads and not two
        vw, n_keys = v1.shape[-1], k1.shape[1] // heads
        kv = lambda a, w: pl.BlockSpec((None, a.shape[1], w), lambda bi, hg, qi: (bi, 0, 0))
    else:
        vw, n_keys = v1.shape[-1] // heads, k1.shape[1]
        kv = lambda a, w: pl.BlockSpec((None, a.shape[1], hb * w), lambda bi, hg, qi: (bi, 0, hg))
    in_specs = [pl.BlockSpec((None, tq, hb * dk), lambda bi, hg, qi: (bi, qi, hg)),
                kv(k1, dk), kv(v1, vw)]
    args = [q, k1, v1]
    if two:
        in_specs += [kv(k2, dk), kv(v2, vw)]
        args += [k2, v2]
    return pl.pallas_call(
        functools.partial(_attn_kernel, hb=hb, dk=dk, dv=dv, vw=vw, two=two,
                          tk=_tile(n_keys, tk), head_rows=heads if head_rows else 0),
        grid=(b, heads // hb, sq // tq),
        in_specs=in_specs,
        out_specs=pl.BlockSpec((None, tq, hb * dv), lambda bi, hg, qi: (bi, qi, hg)),
        out_shape=jax.ShapeDtypeStruct((b, sq, heads * dv), BF16),
        compiler_params=_cp("parallel", "parallel", "arbitrary"),
    )(*args)


def _na_bias_tables(rpb, rows):
    win_r = min(WIN_R_MAX, rows)
    qc = np.arange(GRID_W)[:, None]
    kc = np.arange(GRID_W)[None, :]
    qs = np.clip(qc - WIN_C // 2, 0, GRID_W - WIN_C)
    col_valid = (kc >= qs) & (kc < qs + WIN_C)
    col_off = np.clip(kc - qc + (WIN_C - 1), 0, 2 * WIN_C - 2)
    onehot = (col_off[None] == np.arange(2 * WIN_C - 1)[:, None, None]).astype(np.float32)
    e = jnp.einsum("hrc,cqk->hrqk", rpb, onehot, precision=lax.Precision.HIGHEST)
    e = jnp.where(col_valid, e * LOG2E, NEG_INF)
    n_roff = 2 * WIN_R_MAX - 1
    e = jnp.concatenate([e, jnp.full((H_B, 1, GRID_W, GRID_W), NEG_INF, F32)], axis=1)
    slab = []
    for r_blk in (0, NA_QROWS, rows - NA_QROWS):
        kr0 = int(np.clip(r_blk - win_r // 2, 0, rows - NA_KROWS))
        per_row = []
        for a in range(NA_QROWS):
            r = r_blk + a
            r0 = int(np.clip(r - win_r // 2, 0, rows - win_r))
            per_row.append([kr - r + (WIN_R_MAX - 1) if r0 <= kr < r0 + win_r else n_roff
                            for kr in range(kr0, kr0 + NA_KROWS)])
        slab.append(per_row)

    def assemble(e_ref, o_ref):
        for v, per_row in enumerate(slab):
            for a, idx in enumerate(per_row):
                o_ref[v, a * GRID_W:(a + 1) * GRID_W, :] = jnp.concatenate([e_ref[j] for j in idx], axis=-1)

    return pl.pallas_call(
        assemble,
        grid=(H_B,),
        in_specs=[pl.BlockSpec((None, n_roff + 1, GRID_W, GRID_W), lambda h: (h, 0, 0, 0))],
        out_specs=pl.BlockSpec((None, 3, NA_QROWS * GRID_W, NA_KROWS * GRID_W), lambda h: (h, 0, 0, 0)),
        out_shape=jax.ShapeDtypeStruct((H_B, 3, NA_QROWS * GRID_W, NA_KROWS * GRID_W), F32),
        compiler_params=_cp("parallel"),
    )(e)


def _na_kernel(q_ref, k_ref, v_ref, kc_ref, vc_ref, bias_ref, o_ref, *, rows):
    nblk = rows // NA_QROWS
    nq, nk = NA_QROWS * GRID_W, NA_KROWS * GRID_W
    n_ctx = kc_ref.shape[0] // H_B
    head_rows = pl.ds(pl.program_id(1), n_ctx, stride=H_B)
    kc = kc_ref[head_rows, :].astype(BF16)
    vc = jnp.concatenate([vc_ref[head_rows, :].astype(BF16), jnp.ones((n_ctx, HD_B), BF16)], axis=-1)

    for t in range(nblk):
        r_blk = t * NA_QROWS
        kr0 = int(np.clip(r_blk - min(WIN_R_MAX, rows) // 2, 0, rows - NA_KROWS))
        var = 0 if t == 0 else (2 if t == nblk - 1 else 1)
        q0, k0 = r_blk * GRID_W, kr0 * GRID_W
        q = q_ref[q0:q0 + nq, :]
        s = _dot_nt(q, k_ref[k0:k0 + nk, :]) + bias_ref[var]
        sc = _dot_nt(q, kc)
        m = jnp.maximum(jnp.max(s, axis=-1, keepdims=True), jnp.max(sc, axis=-1, keepdims=True))
        p = jnp.exp2(s - m)
        pc = jnp.exp2(sc - m)
        o = _dot(p.astype(BF16), v_ref[k0:k0 + nk, :]) + _dot(pc.astype(BF16), vc)
        o_ref[q0:q0 + nq, :] = (o[:, :HD_B] * (1.0 / o[:, HD_B:])).astype(o_ref.dtype)


def _na_attention(q, k, v, k_ctx, v_ctx, bias):
    b, s, _ = q.shape
    rows = s // GRID_W
    head = lambda n: pl.BlockSpec((None, n, HD_B), lambda bi, h: (bi, 0, h))
    ctx = pl.BlockSpec((None,) + k_ctx.shape[1:], lambda bi, h: (bi, 0, 0))
    return pl.pallas_call(
        functools.partial(_na_kernel, rows=rows),
        grid=(b, H_B),
        in_specs=[head(s), head(s),
                  pl.BlockSpec((None, s, 2 * HD_B), lambda bi, h: (bi, 0, h)),
                  ctx, ctx,
                  pl.BlockSpec((None,) + bias.shape[1:], lambda bi, h: (h, 0, 0, 0))],
        out_specs=head(s),
        out_shape=jax.ShapeDtypeStruct((b, s, H_B * HD_B), BF16),
        compiler_params=_cp("parallel", "arbitrary"),
    )(q, k, v, k_ctx, v_ctx, bias)


def _merge_kernel(oa_ref, ob_ref, h_ref, woa_ref, wob_ref, wga_ref, wgb_ref, m_ref, *, row_split):
    rs = h_ref.shape[0] // row_split
    for r in range(row_split):
        rows = slice(r * rs, (r + 1) * rs)
        h = h_ref[rows, :]
        ya = _dot(oa_ref[rows, :], woa_ref[...])
        yb = _dot(ob_ref[rows, :], wob_ref[...])
        ga = _sigmoid(_dot(h, wga_ref[...]))
        gb = _sigmoid(_dot(h, wgb_ref[...]))
        m_ref[rows, :] = (ga * ya + gb * yb).astype(BF16)


def _merge(oa, ob, h, w_oa, w_ob, w_ga, w_gb):
    m, d = h.shape
    n = w_oa.shape[1]
    tm = _tile(m, ROW_TILE)
    row = lambda a: pl.BlockSpec((tm, a.shape[1]), lambda i: (i, 0))
    full = lambda a: pl.BlockSpec(a.shape, lambda i: (0, 0), pipeline_mode=pl.Buffered(1))
    return pl.pallas_call(
        functools.partial(_merge_kernel, row_split=2 if tm % 16 == 0 else 1),
        grid=(m // tm,),
        in_specs=[row(oa), row(ob), row(h), full(w_oa), full(w_ob), full(w_ga), full(w_gb)],
        out_specs=pl.BlockSpec((tm, n), lambda i: (i, 0)),
        out_shape=jax.ShapeDtypeStruct((m, n), BF16),
        compiler_params=_cp("parallel"),
    )(oa, ob, h, w_oa, w_ob, w_ga, w_gb)


def _mixout_kernel(m_ref, x_ref, w_ref, gt_ref, g_ref, sh_ref, sc_ref, x1_ref, h2_ref, *, row_split):
    rs = m_ref.shape[0] // row_split
    for r in range(row_split):
        rows = slice(r * rs, (r + 1) * rs)
        x1 = x_ref[rows, :] + gt_ref[...] * _dot(m_ref[rows, :], w_ref[...])
        x1_ref[rows, :] = x1
        h2_ref[rows, :] = (_rms(x1, g_ref[...]) * (1.0 + sc_ref[...]) + sh_ref[...]).astype(BF16)


def _mixout(mm, x, w_out, g2, mod3, row0):
    bm, sm, d = x.shape
    ts = _tile(sm, ROW_TILE)
    tok = pl.BlockSpec((None, ts, d), lambda b, i: (b, i, 0))
    return pl.pallas_call(
        functools.partial(_mixout_kernel, row_split=2 if ts % 16 == 0 else 1),
        grid=(bm, sm // ts),
        in_specs=[tok, tok, pl.BlockSpec((d, d), lambda b, i: (0, 0), pipeline_mode=pl.Buffered(1)),
                  _mod_spec(d, 2, row0, 2), pl.BlockSpec((1, d), lambda b, i: (0, 0)),
                  _mod_spec(d, 3, row0, 2), _mod_spec(d, 4, row0, 2)],
        out_specs=[tok, tok],
        out_shape=[jax.ShapeDtypeStruct((bm, sm, d), F32),
                   jax.ShapeDtypeStruct((bm, sm, d), BF16)],
        compiler_params=_cp("parallel", "parallel"),
    )(mm, x, w_out, mod3, g2.reshape(1, d), mod3, mod3)


def _ffn_up_kernel(h_ref, wg_ref, wu_ref, o_ref, *, row_split):
    rs = h_ref.shape[0] // row_split
    for r in range(row_split):
        h = h_ref[r * rs:(r + 1) * rs, :]
        g = _dot(h, wg_ref[...])
        u = _dot(h, wu_ref[...])
        o_ref[r * rs:(r + 1) * rs, :] = (g * _sigmoid(g) * u).astype(BF16)


def _ffn_up(h2, w_gu):
    m, d = h2.shape
    d_ff = w_gu.shape[1] // 2
    tm = _tile(m, FFN_UP_ROWS)
    tn = d_ff // 2 if d_ff % (2 * MXU_COLS) == 0 else _tile(d_ff, COL_TILE)
    nj = d_ff // tn
    wspec = lambda off: pl.BlockSpec((d, tn), lambda j, i: (0, j + off), pipeline_mode=pl.Buffered(1))
    return pl.pallas_call(
        functools.partial(_ffn_up_kernel, row_split=4 if tm % 32 == 0 else 1),
        grid=(nj, m // tm),
        in_specs=[pl.BlockSpec((tm, d), lambda j, i: (i, 0)), wspec(0), wspec(nj)],
        out_specs=pl.BlockSpec((tm, tn), lambda j, i: (i, j)),
        out_shape=jax.ShapeDtypeStruct((m, d_ff), BF16),
        compiler_params=_cp("arbitrary", "arbitrary"),
    )(h2, w_gu, w_gu)


def _ffn_down_kernel(a_ref, w_ref, x_hbm, gt_ref, g_ref, y_ref, x_buf, x_sem, *, row_split):
    b, i = pl.program_id(0), pl.program_id(1)
    ts, d = x_buf.shape
    x_copy = pltpu.make_async_copy(x_hbm.at[b, pl.ds(i * ts, ts), :], x_buf, x_sem)
    x_copy.start()
    rs = ts // row_split
    tn = _tile(d, COL_TILE)
    for r in range(row_split):
        rows = slice(r * rs, (r + 1) * rs)
        yr = y_ref.at[r] if len(y_ref.shape) == 3 else y_ref.at[pl.ds(r * rs, rs)]
        for n in range(d // tn):
            cols = slice(n * tn, (n + 1) * tn)
            yr[:, cols] = _dot(a_ref[rows, :], w_ref[:, cols])
        if r == 0:
            x_copy.wait()
        yr[...] = _rms(x_buf[rows, :] + gt_ref[...] * yr[...], g_ref[...])


def _ffn_down(hid, w_down, x1, norm_f_g, mod3, row0, out_seq):
    bm, sm, d = x1.shape
    d_ff = w_down.shape[0]
    ts = _tile(sm, ROW_TILE)
    if out_seq >= ts:
        assert out_seq == sm
        row_split = 2 if ts % 16 == 0 else 1
        out_spec = pl.BlockSpec((None, ts, d), lambda b, i: (b, i, 0))
    else:
        assert bm == 1 and ts % out_seq == 0
        row_split = ts // out_seq
        out_spec = pl.BlockSpec((row_split, out_seq, d), lambda b, i: (i, 0, 0))
    return pl.pallas_call(
        functools.partial(_ffn_down_kernel, row_split=row_split),
        grid=(bm, sm // ts),
        in_specs=[pl.BlockSpec((None, ts, d_ff), lambda b, i: (b, i, 0)),
                  pl.BlockSpec((d_ff, d), lambda b, i: (0, 0), pipeline_mode=pl.Buffered(1)),
                  pl.BlockSpec(memory_space=pl.ANY),
                  _mod_spec(d, 5, row0, 2),
                  pl.BlockSpec((1, d), lambda b, i: (0, 0))],
        out_specs=out_spec,
        out_shape=jax.ShapeDtypeStruct((bm * sm // out_seq, out_seq, d), F32),
        scratch_shapes=[pltpu.VMEM((ts, d), F32), pltpu.SemaphoreType.DMA(())],
        compiler_params=_cp("arbitrary", "arbitrary"),
    )(hid, w_down, x1, mod3, norm_f_g.reshape(1, d))


def _rope_tables(n_tokens):
    t = jnp.arange(n_tokens, dtype=jnp.int32)
    row = (t // GRID_W).astype(F32)
    col = (t % GRID_W).astype(F32)
    n_freq = ROPE_DIM // 4
    inv_freq = ROPE_THETA ** (-jnp.arange(n_freq, dtype=F32) / n_freq)
    ang = jnp.concatenate([row[:, None] * inv_freq, col[:, None] * inv_freq], axis=-1)
    cos, sin = jnp.cos(ang), jnp.sin(ang)
    z = jnp.zeros_like(cos)
    return (jnp.concatenate([cos, cos, z, z], axis=-1),
            jnp.concatenate([-sin, z, z, z], axis=-1),
            jnp.concatenate([z, sin, z, z], axis=-1))


def _layer(x, mod3, row0, w, rope_tabs, attend, out_seq):
    bm, sm, d = x.shape
    m = bm * sm
    h, q_b, k_b, v_b = _qkv(x, w["norm1_g"], mod3, row0, w["w_q"], w["w_k"], w["w_v"], attend.cache_layout)
    h2d = h.reshape(m, d)
    q_a, ckv, kr128, k_a, v_a = _latent(h, w["w_lat"], w["q_norm_g"], w["kv_norm_g"], w["w_uq_p"],
                                        w["w_uk"], w["w_uv"], rope_tabs)
    o_a, o_b = attend(q_a, k_a, v_a, q_b, k_b, v_b)
    mm = _merge(o_a.reshape(m, -1), o_b.reshape(m, -1), h2d, w["w_oa"], w["w_ob"], w["w_ga"], w["w_gb"])
    x1, hn = _mixout(mm.reshape(bm, sm, d), x, w["w_out"], w["norm2_g"], mod3, row0)
    hid = _ffn_up(hn.reshape(m, d), w["w_gu"])
    x2 = _ffn_down(hid.reshape(bm, sm, -1), w["w_down"], x1, w["norm_f_g"], mod3, row0, out_seq)
    return x2, ckv, kr128, k_b, v_b


class _PromptAttend:
    cache_layout = True

    def __init__(self, batch, seq):
        self.batch, self.seq = batch, seq

    def __call__(self, q_a, k_a, v_a, q_b, k_b, v_b):
        sh = lambda a: a.reshape(self.batch, self.seq, -1)
        o_a = _attention(sh(q_a), sh(k_a), sh(v_a), None, None, heads=H_A, dv=V_DIM, hb=H_A, tq=self.seq)
        rows = lambda a: a.reshape(self.batch, self.seq * H_B, HD_B)
        o_b = _attention(sh(q_b), rows(k_b), rows(v_b), None, None, heads=H_B, dv=HD_B, hb=H_B,
                         tq=self.seq, head_rows=True)
        return o_a, o_b


class _SampleAttend:
    cache_layout = False

    def __init__(self, k_ctx_a, v_ctx_a, k_ctx_b, v_ctx_b, bias):
        self.ctx = (k_ctx_a, v_ctx_a, k_ctx_b, v_ctx_b, bias)

    def __call__(self, q_a, k_a, v_a, q_b, k_b, v_b):
        k_ctx_a, v_ctx_a, k_ctx_b, v_ctx_b, bias = self.ctx
        o_a = _attention(q_a, k_a, v_a, k_ctx_a, v_ctx_a, heads=H_A, dv=V_DIM, hb=1, tq=MLA_TQ, tk=MLA_TK)
        o_b = _na_attention(q_b, k_b, v_b, k_ctx_b, v_ctx_b, bias)
        return o_a, o_b


def kernel(x_prompt, x_sample, cache_mla_ckv, cache_mla_krope, cache_na_k, cache_na_v, c, c_ctx,
           w_mod, b_mod, norm1_g, w_in, q_norm_g, kv_norm_g, w_uq, w_uk, w_uv, rpb,
           w_oa, w_ob, w_out, norm2_g, w_gu, w_down, norm_f_g):
    batch, seq, d = x_prompt.shape
    dec_batch, dec_seq, _ = x_sample.shape
    depth = w_mod.shape[0]
    assert depth == 1, "one trunk layer: the final norm is fused into the layer's last kernel"
    past = cache_mla_ckv.shape[2]
    q_lora, kv_lora = q_norm_g.shape[-1], kv_norm_g.shape[-1]
    na_w = H_B * HD_B
    rows = dec_seq // GRID_W
    assert rows % NA_QROWS == 0 and rows >= NA_KROWS

    n_cond = 1 + dec_batch
    r8 = -(-n_cond // 8) * 8
    cond = jnp.zeros((r8, d), F32).at[0].set(c_ctx).at[1:n_cond].set(c)

    l = 0
    mod3 = _adaln(cond, w_mod[l], b_mod[l]).reshape(r8, 1, 6 * d)

    wi = w_in[l]
    o = 0
    sec = {}
    for name, width in (("lat", q_lora + kv_lora + ROPE_DIM), ("q", na_w), ("k", na_w), ("v", na_w),
                        ("ga", d), ("gb", d)):
        sec[name] = wi[:, o:o + width]
        o += width
    lat_w = q_lora + kv_lora + LANES
    w_uq_p = jnp.pad(w_uq[l].reshape(q_lora, H_A, NOPE_DIM + ROPE_DIM),
                     ((0, 0), (0, 0), (0, QK_SLOT - NOPE_DIM - ROPE_DIM))).reshape(q_lora, H_A * QK_SLOT)
    w = {
        "w_lat": jnp.pad(sec["lat"], ((0, 0), (0, lat_w - sec["lat"].shape[1]))).astype(BF16),
        "w_q": sec["q"].astype(BF16), "w_k": sec["k"].astype(BF16), "w_v": sec["v"].astype(BF16),
        "w_ga": sec["ga"].astype(BF16), "w_gb": sec["gb"].astype(BF16),
        "w_uq_p": w_uq_p.astype(BF16), "w_uk": w_uk[l].astype(BF16), "w_uv": w_uv[l].astype(BF16),
        "w_oa": w_oa[l].astype(BF16), "w_ob": w_ob[l].astype(BF16), "w_out": w_out[l].astype(BF16),
        "w_gu": w_gu[l].astype(BF16), "w_down": w_down[l].astype(BF16),
        "norm1_g": norm1_g[l], "norm2_g": norm2_g[l], "q_norm_g": q_norm_g[l], "kv_norm_g": kv_norm_g[l],
        "norm_f_g": norm_f_g,
    }

    xp = x_prompt.reshape(1, batch * seq, d)
    yp, ckv_p, kr_p, k_b_p, v_b_p = _layer(xp, mod3, 0, w, None, _PromptAttend(batch, seq), seq)

    kr_ctx = jnp.pad(cache_mla_krope[:, l], ((0, 0), (0, 0), (0, LANES - ROPE_DIM)))
    k_ctx_a, v_ctx_a = _kvexp(cache_mla_ckv[:, l], kr_ctx, w["w_uk"], w["w_uv"])
    attend = _SampleAttend(k_ctx_a, v_ctx_a,
                           cache_na_k[:, l].reshape(dec_batch, past * H_B, HD_B),
                           cache_na_v[:, l].reshape(dec_batch, past * H_B, HD_B),
                           _na_bias_tables(rpb[l], rows))
    ys, _, _, _, _ = _layer(x_sample, mod3, 1, w, _rope_tables(dec_seq), attend, dec_seq)

    return (yp, ys,
            ckv_p.reshape(batch, 1, seq, kv_lora),
            kr_p.reshape(batch, seq, LANES)[:, :, :ROPE_DIM].reshape(batch, 1, seq, ROPE_DIM),
            k_b_p.reshape(batch, 1, seq, H_B, HD_B),
            v_b_p.reshape(batch, 1, seq, H_B, HD_B))
```

```python
import functools

import jax
import jax.numpy as jnp
import numpy as np
from jax import lax
from jax.experimental import pallas as pl
from jax.experimental.pallas import tpu as pltpu

F32 = jnp.float32
BF16 = jnp.bfloat16

GRID_W = 64
H_A = 8
NOPE_DIM = 128
ROPE_DIM = 64
V_DIM = 128
H_B = 8
HD_B = 128
WIN_R_MAX = 8
WIN_C = 16
ROPE_THETA = 10000.0
NORM_EPS = 1e-6
NEG_INF = -1e30
LOG2E = 1.4426950408889634
MLA_QSCALE = (NOPE_DIM + ROPE_DIM) ** -0.5 * LOG2E
NA_QSCALE = HD_B ** -0.5 * LOG2E

LANES = 128
MXU_COLS = 256
QK_SLOT = 256
NA_QROWS = 4
NA_KROWS = NA_QROWS + WIN_R_MAX
VMEM_LIMIT = 56 * 2 ** 20

ROW_TILE = 512
FFN_UP_ROWS = 1024
COL_TILE = 512
ADALN_COLS = 1024
MLA_TQ, MLA_TK = 1024, 512
ATTN_TK = 1024


def _cp(*sem):
    return pltpu.CompilerParams(dimension_semantics=sem, vmem_limit_bytes=VMEM_LIMIT)


def _dot(a, b):
    return jnp.dot(a, b, preferred_element_type=F32)


def _dot_nt(a, b):
    return lax.dot_general(a, b, (((1,), (1,)), ((), ())), preferred_element_type=F32)


def _sigmoid(x):
    return 1.0 / (1.0 + jnp.exp(-x))


def _rms(x, g):
    return x * lax.rsqrt(jnp.mean(x * x, axis=-1, keepdims=True) + NORM_EPS) * g


def _tile(n, want):
    t = min(n, want)
    while n % t:
        t //= 2
    return t


def _adaln_kernel(c_ref, w_ref, b_ref, o_ref):
    c = c_ref[...]
    s = (c * _sigmoid(c)).astype(BF16)
    o_ref[...] = _dot(s, w_ref[...].astype(BF16)) + b_ref[...]


def _adaln(cond, w_mod, b_mod):
    r, d = cond.shape
    n = w_mod.shape[1]
    tn = _tile(n, ADALN_COLS)
    return pl.pallas_call(
        _adaln_kernel,
        grid=(n // tn,),
        in_specs=[pl.BlockSpec((r, d), lambda j: (0, 0)),
                  pl.BlockSpec((d, tn), lambda j: (0, j)),
                  pl.BlockSpec((1, tn), lambda j: (0, j))],
        out_specs=pl.BlockSpec((r, tn), lambda j: (0, j)),
        out_shape=jax.ShapeDtypeStruct((r, n), F32),
        compiler_params=_cp("arbitrary"),
    )(cond, w_mod, b_mod.reshape(1, n))


def _col_view_spec(view, grid_rank):
    arr, width, idx = view
    index_map = (lambda i: (0, idx)) if grid_rank == 1 else (lambda b, i: (0, idx))
    return pl.BlockSpec((arr.shape[0], width), index_map, pipeline_mode=pl.Buffered(1))


def _mod_spec(d, sec, row0, grid_rank):
    if grid_rank == 2:
        return pl.BlockSpec((None, 1, d), lambda b, i: (row0 + b, 0, sec))
    return pl.BlockSpec((None, 1, d), lambda b, i, j: (row0 + b, 0, sec))


def _qkv_kernel(x_ref, g_ref, sh_ref, sc_ref, wq_ref, wk_ref, wv_ref,
                h_ref, q_ref, k_ref, v_ref, *, row_split, cache_layout):
    rs = x_ref.shape[0] // row_split
    for r in range(row_split):
        rows = slice(r * rs, (r + 1) * rs)
        h = (_rms(x_ref[rows, :], g_ref[...]) * (1.0 + sc_ref[...]) + sh_ref[...]).astype(BF16)
        h_ref[rows, :] = h
        q_ref[rows, :] = (_dot(h, wq_ref[...]) * NA_QSCALE).astype(BF16)
        k = _dot(h, wk_ref[...])
        v = _dot(h, wv_ref[...])
        if cache_layout:
            for hh in range(H_B):
                head_rows = pl.ds(r * rs * H_B + hh, rs, stride=H_B)
                k_ref[head_rows, :] = k[:, hh * HD_B:(hh + 1) * HD_B]
                v_ref[head_rows, :] = v[:, hh * HD_B:(hh + 1) * HD_B]
        else:
            k_ref[rows, :] = k.astype(BF16)
            one = jnp.ones((rs, HD_B), BF16)
            for hh in range(H_B):
                v_ref[rows, 2 * hh * HD_B:(2 * hh + 1) * HD_B] = v[:, hh * HD_B:(hh + 1) * HD_B].astype(BF16)
                v_ref[rows, (2 * hh + 1) * HD_B:(2 * hh + 2) * HD_B] = one


def _qkv(x, g, mod3, row0, wq, wk, wv, cache_layout):
    bm, sm, d = x.shape
    n = wq[1]
    ts = _tile(sm, ROW_TILE)
    tok = lambda w: pl.BlockSpec((None, ts, w), lambda b, i: (b, i, 0))
    if cache_layout:
        kv_specs = [pl.BlockSpec((None, ts * H_B, HD_B), lambda b, i: (b, i, 0))] * 2
        kv_shapes = [jax.ShapeDtypeStruct((bm, sm * H_B, HD_B), F32)] * 2
    else:
        kv_specs = [tok(n), tok(2 * n)]
        kv_shapes = [jax.ShapeDtypeStruct((bm, sm, n), BF16), jax.ShapeDtypeStruct((bm, sm, 2 * n), BF16)]
    return pl.pallas_call(
        functools.partial(_qkv_kernel, row_split=2 if ts % 16 == 0 else 1, cache_layout=cache_layout),
        grid=(bm, sm // ts),
        in_specs=[tok(d), pl.BlockSpec((1, d), lambda b, i: (0, 0)),
                  _mod_spec(d, 0, row0, 2), _mod_spec(d, 1, row0, 2)]
                 + [_col_view_spec(v, 2) for v in (wq, wk, wv)],
        out_specs=[tok(d), tok(n)] + kv_specs,
        out_shape=[jax.ShapeDtypeStruct((bm, sm, d), BF16), jax.ShapeDtypeStruct((bm, sm, n), BF16)] + kv_shapes,
        compiler_params=_cp("parallel", "parallel"),
    )(x, g.reshape(1, d), mod3, mod3, wq[0], wk[0], wv[0])


def _rope128(g, c, s1, s2):
    return g * c + pltpu.roll(g, 96, 1) * s1 + pltpu.roll(g, 32, 1) * s2


def _expand_kv(ckv, kr, wuk_ref, wuv_ref, k_ref, v_ref):
    c = ckv.astype(BF16)
    kn = _dot(c, wuk_ref[...])
    v = _dot(c, wuv_ref[...])
    krb = kr.astype(BF16)
    one = jnp.ones((c.shape[0], V_DIM), BF16)
    for hh in range(H_A):
        lo = hh * QK_SLOT
        k_ref[:, lo:lo + LANES] = kn[:, hh * NOPE_DIM:(hh + 1) * NOPE_DIM].astype(BF16)
        k_ref[:, lo + LANES:lo + QK_SLOT] = krb
        v_ref[:, 2 * hh * V_DIM:(2 * hh + 1) * V_DIM] = v[:, hh * V_DIM:(hh + 1) * V_DIM].astype(BF16)
        v_ref[:, (2 * hh + 1) * V_DIM:(2 * hh + 2) * V_DIM] = one


def _latent_kernel(*refs, q_lora, kv_lora, rope):
    if rope:
        (h_ref, wl_ref, qg_ref, kg_ref, wuq_ref, wuk_ref, wuv_ref, c_ref, s1_ref, s2_ref,
         q_ref, ckv_ref, kr_ref, k_ref, v_ref) = refs
    else:
        (h_ref, wl_ref, qg_ref, kg_ref, wuq_ref, wuk_ref, wuv_ref,
         q_ref, ckv_ref, kr_ref, k_ref, v_ref) = refs
    lat = _dot(h_ref[...], wl_ref[...])
    cq = lat[:, :q_lora]
    ckv = _rms(lat[:, q_lora:q_lora + kv_lora], kg_ref[...])
    kr = lat[:, q_lora + kv_lora:]
    ckv_ref[...] = ckv
    if rope:
        c, s1, s2 = c_ref[...], s1_ref[...], s2_ref[...]
        kr = _rope128(kr, c, s1, s2)
    kr_ref[...] = kr
    _expand_kv(ckv, kr, wuk_ref, wuv_ref, k_ref, v_ref)
    qa = _dot(_rms(cq, qg_ref[...]).astype(BF16), wuq_ref[...])
    for hh in range(H_A):
        lo = hh * QK_SLOT
        q_ref[:, lo:lo + LANES] = (qa[:, lo:lo + LANES] * MLA_QSCALE).astype(BF16)
        g = qa[:, lo + LANES:lo + QK_SLOT]
        if rope:
            g = _rope128(g, c, s1, s2)
        q_ref[:, lo + LANES:lo + QK_SLOT] = (g * MLA_QSCALE).astype(BF16)


def _latent(h, w_lat, q_norm_g, kv_norm_g, w_uq_p, w_uk, w_uv, rope_tabs):
    bm, sm, d = h.shape
    q_lora, kv_lora = q_norm_g.shape[-1], kv_norm_g.shape[-1]
    ts = _tile(sm, ROW_TILE)
    rope = rope_tabs is not None
    tok = lambda w: pl.BlockSpec((None, ts, w), lambda b, i: (b, i, 0))
    full = lambda a: pl.BlockSpec(a.shape, lambda b, i: (0, 0))
    qg, kg = q_norm_g.reshape(1, q_lora), kv_norm_g.reshape(1, kv_lora)
    args = [h, w_lat[0], qg, kg, w_uq_p, w_uk, w_uv]
    in_specs = [tok(d), _col_view_spec(w_lat, 2)] + [full(a) for a in args[2:]]
    if rope:
        in_specs += [pl.BlockSpec((ts, LANES), lambda b, i: (i, 0))] * 3
        args += list(rope_tabs)
    widths = (H_A * QK_SLOT, kv_lora, LANES, H_A * QK_SLOT, H_A * 2 * V_DIM)
    dtypes = (BF16, F32, F32, BF16, BF16)
    return pl.pallas_call(
        functools.partial(_latent_kernel, q_lora=q_lora, kv_lora=kv_lora, rope=rope),
        grid=(bm, sm // ts),
        in_specs=in_specs,
        out_specs=[tok(w) for w in widths],
        out_shape=[jax.ShapeDtypeStruct((bm, sm, w), t) for w, t in zip(widths, dtypes)],
        compiler_params=_cp("parallel", "parallel"),
    )(*args)


def _kvexp_kernel(ckv_ref, kr_ref, wuk_ref, wuv_ref, k_ref, v_ref):
    _expand_kv(ckv_ref[...], kr_ref[...], wuk_ref, wuv_ref, k_ref, v_ref)


def _kvexp(ckv, kr128, w_uk, w_uv):
    bm, sm, kv_lora = ckv.shape
    ts = _tile(sm, ROW_TILE)
    tok = lambda w: pl.BlockSpec((None, ts, w), lambda b, i: (b, i, 0))
    full = lambda a: pl.BlockSpec(a.shape, lambda b, i: (0, 0))
    widths = (H_A * QK_SLOT, H_A * 2 * V_DIM)
    return pl.pallas_call(
        _kvexp_kernel,
        grid=(bm, sm // ts),
        in_specs=[tok(kv_lora), tok(LANES), full(w_uk), full(w_uv)],
        out_specs=[tok(w) for w in widths],
        out_shape=[jax.ShapeDtypeStruct((bm, sm, w), BF16) for w in widths],
        compiler_params=_cp("parallel", "parallel"),
    )(ckv, kr128, w_uk, w_uv)


def _attn_kernel(*refs, hb, dk, dv, vw, two, tk, head_rows):
    ones = vw == 2 * dv
    if two:
        q_ref, k1_ref, v1_ref, k2_ref, v2_ref, o_ref = refs
    else:
        q_ref, k1_ref, v1_ref, o_ref = refs
    n_keys = k1_ref.shape[0] // max(head_rows, 1)
    chunks = [(k1_ref, v1_ref, c * tk, tk) for c in range(n_keys // tk)]
    if two:
        chunks.append((k2_ref, v2_ref, 0, k2_ref.shape[0]))

    def head_slab(ref, lo, n, j, w):
        if head_rows:
            return ref[pl.ds(lo * head_rows + j, n, stride=head_rows), :]
        return ref[lo:lo + n, j * w:(j + 1) * w]

    for j in range(hb):
        q = q_ref[:, j * dk:(j + 1) * dk]

        def logits(ch):
            k_ref, _, lo, n = ch
            return _dot_nt(q, head_slab(k_ref, lo, n, j, dk).astype(BF16))

        s_next = logits(chunks[0])
        m = l = acc = None
        for i, (_, v_ref, lo, n) in enumerate(chunks):
            s = s_next
            if i + 1 < len(chunks):
                s_next = logits(chunks[i + 1])
            mc = jnp.max(s, axis=-1, keepdims=True)
            m_new = mc if m is None else jnp.maximum(m, mc)
            p = jnp.exp2(s - m_new)
            pv = _dot(p.astype(BF16), head_slab(v_ref, lo, n, j, vw).astype(BF16))
            if not ones:
                ps = jnp.sum(p, axis=-1, keepdims=True)
            if m is None:
                acc = pv
                l = None if ones else ps
            else:
                alpha = jnp.exp2(m - m_new)
                acc = alpha * acc + pv
                l = None if ones else alpha * l + ps
            m = m_new
        if ones:
            o = acc[:, :dv] * (1.0 / acc[:, dv:])
        else:
            o = acc * (1.0 / l)
        o_ref[:, j * dv:(j + 1) * dv] = o.astype(o_ref.dtype)


def _attention(q, k1, v1, k2, v2, *, heads, dv, hb, tq, tk=ATTN_TK, head_rows=False):
    b, sq, qw = q.shape
    dk = qw // heads
    two = k2 is not None
    tq = _tile(sq, tq)
    if head_rows:
        assert hb == heads and not two
        vw, n_keys = v1.shape[-1], k1.shape[1] // heads
        kv = lambda a, w: pl.BlockSpec((None, a.shape[1], w), lambda bi, hg, qi: (bi, 0, 0))
    else:
        vw, n_keys = v1.shape[-1] // heads, k1.shape[1]
        kv = lambda a, w: pl.BlockSpec((None, a.shape[1], hb * w), lambda bi, hg, qi: (bi, 0, hg))
    in_specs = [pl.BlockSpec((None, tq, hb * dk), lambda bi, hg, qi: (bi, qi, hg)),
                kv(k1, dk), kv(v1, vw)]
    args = [q, k1, v1]
    if two:
        in_specs += [kv(k2, dk), kv(v2, vw)]
        args += [k2, v2]
    return pl.pallas_call(
        functools.partial(_attn_kernel, hb=hb, dk=dk, dv=dv, vw=vw, two=two,
                          tk=_tile(n_keys, tk), head_rows=heads if head_rows else 0),
        grid=(b, heads // hb, sq // tq),
        in_specs=in_specs,
        out_specs=pl.BlockSpec((None, tq, hb * dv), lambda bi, hg, qi: (bi, qi, hg)),
        out_shape=jax.ShapeDtypeStruct((b, sq, heads * dv), BF16),
        compiler_params=_cp("parallel", "parallel", "arbitrary"),
    )(*args)


def _na_bias_tables(rpb, rows):
    win_r = min(WIN_R_MAX, rows)
    qc = np.arange(GRID_W)[:, None]
    kc = np.arange(GRID_W)[None, :]
    qs = np.clip(qc - WIN_C // 2, 0, GRID_W - WIN_C)
    col_valid = (kc >= qs) & (kc < qs + WIN_C)
    col_off = np.clip(kc - qc + (WIN_C - 1), 0, 2 * WIN_C - 2)
    onehot = (col_off[None] == np.arange(2 * WIN_C - 1)[:, None, None]).astype(np.float32)
    e = jnp.einsum("hrc,cqk->hrqk", rpb, onehot, precision=lax.Precision.HIGHEST)
    e = jnp.where(col_valid, e * LOG2E, NEG_INF)
    n_roff = 2 * WIN_R_MAX - 1
    e = jnp.concatenate([e, jnp.full((H_B, 1, GRID_W, GRID_W), NEG_INF, F32)], axis=1)
    slab = []
    for r_blk in (0, NA_QROWS, rows - NA_QROWS):
        kr0 = int(np.clip(r_blk - win_r // 2, 0, rows - NA_KROWS))
        per_row = []
        for a in range(NA_QROWS):
            r = r_blk + a
            r0 = int(np.clip(r - win_r // 2, 0, rows - win_r))
            per_row.append([kr - r + (WIN_R_MAX - 1) if r0 <= kr < r0 + win_r else n_roff
                            for kr in range(kr0, kr0 + NA_KROWS)])
        slab.append(per_row)

    def assemble(e_ref, o_ref):
        for v, per_row in enumerate(slab):
            for a, idx in enumerate(per_row):
                o_ref[v, a * GRID_W:(a + 1) * GRID_W, :] = jnp.concatenate([e_ref[j] for j in idx], axis=-1)

    return pl.pallas_call(
        assemble,
        grid=(H_B,),
        in_specs=[pl.BlockSpec((None, n_roff + 1, GRID_W, GRID_W), lambda h: (h, 0, 0, 0))],
        out_specs=pl.BlockSpec((None, 3, NA_QROWS * GRID_W, NA_KROWS * GRID_W), lambda h: (h, 0, 0, 0)),
        out_shape=jax.ShapeDtypeStruct((H_B, 3, NA_QROWS * GRID_W, NA_KROWS * GRID_W), F32),
        compiler_params=_cp("parallel"),
    )(e)


def _na_kernel(q_ref, k_ref, v_ref, kc_ref, vc_ref, bias_ref, o_ref, *, rows):
    nblk = rows // NA_QROWS
    nq, nk = NA_QROWS * GRID_W, NA_KROWS * GRID_W
    n_ctx = kc_ref.shape[0] // H_B
    head_rows = pl.ds(pl.program_id(1), n_ctx, stride=H_B)
    kc = kc_ref[head_rows, :].astype(BF16)
    vc = jnp.concatenate([vc_ref[head_rows, :].astype(BF16), jnp.ones((n_ctx, HD_B), BF16)], axis=-1)

    for t in range(nblk):
        r_blk = t * NA_QROWS
        kr0 = int(np.clip(r_blk - min(WIN_R_MAX, rows) // 2, 0, rows - NA_KROWS))
        var = 0 if t == 0 else (2 if t == nblk - 1 else 1)
        q0, k0 = r_blk * GRID_W, kr0 * GRID_W
        q = q_ref[q0:q0 + nq, :]
        s = _dot_nt(q, k_ref[k0:k0 + nk, :]) + bias_ref[var]
        sc = _dot_nt(q, kc)
        m = jnp.maximum(jnp.max(s, axis=-1, keepdims=True), jnp.max(sc, axis=-1, keepdims=True))
        p = jnp.exp2(s - m)
        pc = jnp.exp2(sc - m)
        o = _dot(p.astype(BF16), v_ref[k0:k0 + nk, :]) + _dot(pc.astype(BF16), vc)
        o_ref[q0:q0 + nq, :] = (o[:, :HD_B] * (1.0 / o[:, HD_B:])).astype(o_ref.dtype)


def _na_attention(q, k, v, k_ctx, v_ctx, bias):
    b, s, _ = q.shape
    rows = s // GRID_W
    head = lambda n: pl.BlockSpec((None, n, HD_B), lambda bi, h: (bi, 0, h))
    ctx = pl.BlockSpec((None,) + k_ctx.shape[1:], lambda bi, h: (bi, 0, 0))
    return pl.pallas_call(
        functools.partial(_na_kernel, rows=rows),
        grid=(b, H_B),
        in_specs=[head(s), head(s),
                  pl.BlockSpec((None, s, 2 * HD_B), lambda bi, h: (bi, 0, h)),
                  ctx, ctx,
                  pl.BlockSpec((None,) + bias.shape[1:], lambda bi, h: (h, 0, 0, 0))],
        out_specs=head(s),
        out_shape=jax.ShapeDtypeStruct((b, s, H_B * HD_B), BF16),
        compiler_params=_cp("parallel", "arbitrary"),
    )(q, k, v, k_ctx, v_ctx, bias)


def _merge_kernel(oa_ref, ob_ref, h_ref, woa_ref, wob_ref, wga_ref, wgb_ref, m_ref, *, row_split):
    rs = h_ref.shape[0] // row_split
    for r in range(row_split):
        rows = slice(r * rs, (r + 1) * rs)
        h = h_ref[rows, :]
        ya = _dot(oa_ref[rows, :], woa_ref[...])
        yb = _dot(ob_ref[rows, :], wob_ref[...])
        ga = _sigmoid(_dot(h, wga_ref[...]))
        gb = _sigmoid(_dot(h, wgb_ref[...]))
        m_ref[rows, :] = (ga * ya + gb * yb).astype(BF16)


def _merge(oa, ob, h, w_oa, w_ob, w_ga, w_gb):
    m, d = h.shape
    n = w_oa.shape[1]
    tm = _tile(m, ROW_TILE)
    row = lambda a: pl.BlockSpec((tm, a.shape[1]), lambda i: (i, 0))
    full = lambda a: pl.BlockSpec(a.shape, lambda i: (0, 0), pipeline_mode=pl.Buffered(1))
    return pl.pallas_call(
        functools.partial(_merge_kernel, row_split=2 if tm % 16 == 0 else 1),
        grid=(m // tm,),
        in_specs=[row(oa), row(ob), row(h), full(w_oa), full(w_ob),
                  _col_view_spec(w_ga, 1), _col_view_spec(w_gb, 1)],
        out_specs=pl.BlockSpec((tm, n), lambda i: (i, 0)),
        out_shape=jax.ShapeDtypeStruct((m, n), BF16),
        compiler_params=_cp("parallel"),
    )(oa, ob, h, w_oa, w_ob, w_ga[0], w_gb[0])


def _mixout_kernel(m_ref, x_ref, w_ref, gt_ref, g_ref, sh_ref, sc_ref, x1_ref, h2_ref, *, row_split):
    rs = m_ref.shape[0] // row_split
    for r in range(row_split):
        rows = slice(r * rs, (r + 1) * rs)
        x1 = x_ref[rows, :] + gt_ref[...] * _dot(m_ref[rows, :], w_ref[...])
        x1_ref[rows, :] = x1
        h2_ref[rows, :] = (_rms(x1, g_ref[...]) * (1.0 + sc_ref[...]) + sh_ref[...]).astype(BF16)


def _mixout(mm, x, w_out, g2, mod3, row0):
    bm, sm, d = x.shape
    ts = _tile(sm, ROW_TILE)
    tok = pl.BlockSpec((None, ts, d), lambda b, i: (b, i, 0))
    return pl.pallas_call(
        functools.partial(_mixout_kernel, row_split=2 if ts % 16 == 0 else 1),
        grid=(bm, sm // ts),
        in_specs=[tok, tok, pl.BlockSpec((d, d), lambda b, i: (0, 0), pipeline_mode=pl.Buffered(1)),
                  _mod_spec(d, 2, row0, 2), pl.BlockSpec((1, d), lambda b, i: (0, 0)),
                  _mod_spec(d, 3, row0, 2), _mod_spec(d, 4, row0, 2)],
        out_specs=[tok, tok],
        out_shape=[jax.ShapeDtypeStruct((bm, sm, d), F32),
                   jax.ShapeDtypeStruct((bm, sm, d), BF16)],
        compiler_params=_cp("parallel", "parallel"),
    )(mm, x, w_out, mod3, g2.reshape(1, d), mod3, mod3)


def _ffn_up_kernel(h_ref, wg_ref, wu_ref, o_ref, *, row_split):
    rs = h_ref.shape[0] // row_split
    for r in range(row_split):
        h = h_ref[r * rs:(r + 1) * rs, :]
        g = _dot(h, wg_ref[...])
        u = _dot(h, wu_ref[...])
        o_ref[r * rs:(r + 1) * rs, :] = (g * _sigmoid(g) * u).astype(BF16)


def _ffn_up(h2, w_gu):
    m, d = h2.shape
    d_ff = w_gu.shape[1] // 2
    tm = _tile(m, FFN_UP_ROWS)
    tn = d_ff // 2 if d_ff % (2 * MXU_COLS) == 0 else _tile(d_ff, COL_TILE)
    nj = d_ff // tn
    wspec = lambda off: pl.BlockSpec((d, tn), lambda j, i: (0, j + off), pipeline_mode=pl.Buffered(1))
    return pl.pallas_call(
        functools.partial(_ffn_up_kernel, row_split=4 if tm % 32 == 0 else 1),
        grid=(nj, m // tm),
        in_specs=[pl.BlockSpec((tm, d), lambda j, i: (i, 0)), wspec(0), wspec(nj)],
        out_specs=pl.BlockSpec((tm, tn), lambda j, i: (i, j)),
        out_shape=jax.ShapeDtypeStruct((m, d_ff), BF16),
        compiler_params=_cp("arbitrary", "arbitrary"),
    )(h2, w_gu, w_gu)


def _ffn_down_kernel(a_ref, w_ref, x_hbm, gt_ref, g_ref, y_ref, x_buf, x_sem, *, row_split):
    b, i = pl.program_id(0), pl.program_id(1)
    ts, d = x_buf.shape
    x_copy = pltpu.make_async_copy(x_hbm.at[b, pl.ds(i * ts, ts), :], x_buf, x_sem)
    x_copy.start()
    rs = ts // row_split
    tn = _tile(d, COL_TILE)
    for r in range(row_split):
        rows = slice(r * rs, (r + 1) * rs)
        yr = y_ref.at[r] if len(y_ref.shape) == 3 else y_ref.at[pl.ds(r * rs, rs)]
        for n in range(d // tn):
            cols = slice(n * tn, (n + 1) * tn)
            yr[:, cols] = _dot(a_ref[rows, :], w_ref[:, cols])
        if r == 0:
            x_copy.wait()
        yr[...] = _rms(x_buf[rows, :] + gt_ref[...] * yr[...], g_ref[...])


def _ffn_down(hid, w_down, x1, norm_f_g, mod3, row0, out_seq):
    bm, sm, d = x1.shape
    d_ff = w_down.shape[0]
    ts = _tile(sm, ROW_TILE)
    if out_seq >= ts:
        assert out_seq == sm
        row_split = 2 if ts % 16 == 0 else 1
        out_spec = pl.BlockSpec((None, ts, d), lambda b, i: (b, i, 0))
    else:
        assert bm == 1 and ts % out_seq == 0
        row_split = ts // out_seq
        out_spec = pl.BlockSpec((row_split, out_seq, d), lambda b, i: (i, 0, 0))
    return pl.pallas_call(
        functools.partial(_ffn_down_kernel, row_split=row_split),
        grid=(bm, sm // ts),
        in_specs=[pl.BlockSpec((None, ts, d_ff), lambda b, i: (b, i, 0)),
                  pl.BlockSpec((d_ff, d), lambda b, i: (0, 0), pipeline_mode=pl.Buffered(1)),
                  pl.BlockSpec(memory_space=pl.ANY),
                  _mod_spec(d, 5, row0, 2),
                  pl.BlockSpec((1, d), lambda b, i: (0, 0))],
        out_specs=out_spec,
        out_shape=jax.ShapeDtypeStruct((bm * sm // out_seq, out_seq, d), F32),
        scratch_shapes=[pltpu.VMEM((ts, d), F32), pltpu.SemaphoreType.DMA(())],
        compiler_params=_cp("arbitrary", "arbitrary"),
    )(hid, w_down, x1, mod3, norm_f_g.reshape(1, d))


def _rope_tables(n_tokens):
    t = jnp.arange(n_tokens, dtype=jnp.int32)
    row = (t // GRID_W).astype(F32)
    col = (t % GRID_W).astype(F32)
    n_freq = ROPE_DIM // 4
    inv_freq = ROPE_THETA ** (-jnp.arange(n_freq, dtype=F32) / n_freq)
    ang = jnp.concatenate([row[:, None] * inv_freq, col[:, None] * inv_freq], axis=-1)
    cos, sin = jnp.cos(ang), jnp.sin(ang)
    z = jnp.zeros_like(cos)
    return (jnp.concatenate([cos, cos, z, z], axis=-1),
            jnp.concatenate([-sin, z, z, z], axis=-1),
            jnp.concatenate([z, sin, z, z], axis=-1))


def _layer(x, mod3, row0, w, rope_tabs, attend, out_seq):
    bm, sm, d = x.shape
    m = bm * sm
    h, q_b, k_b, v_b = _qkv(x, w["norm1_g"], mod3, row0, w["w_q"], w["w_k"], w["w_v"], attend.cache_layout)
    h2d = h.reshape(m, d)
    q_a, ckv, kr128, k_a, v_a = _latent(h, w["w_lat"], w["q_norm_g"], w["kv_norm_g"], w["w_uq_p"],
                                        w["w_uk"], w["w_uv"], rope_tabs)
    o_a, o_b = attend(q_a, k_a, v_a, q_b, k_b, v_b)
    mm = _merge(o_a.reshape(m, -1), o_b.reshape(m, -1), h2d, w["w_oa"], w["w_ob"], w["w_ga"], w["w_gb"])
    x1, hn = _mixout(mm.reshape(bm, sm, d), x, w["w_out"], w["norm2_g"], mod3, row0)
    hid = _ffn_up(hn.reshape(m, d), w["w_gu"])
    x2 = _ffn_down(hid.reshape(bm, sm, -1), w["w_down"], x1, w["norm_f_g"], mod3, row0, out_seq)
    return x2, ckv, kr128, k_b, v_b


class _PromptAttend:
    cache_layout = True

    def __init__(self, batch, seq):
        self.batch, self.seq = batch, seq

    def __call__(self, q_a, k_a, v_a, q_b, k_b, v_b):
        sh = lambda a: a.reshape(self.batch, self.seq, -1)
        o_a = _attention(sh(q_a), sh(k_a), sh(v_a), None, None, heads=H_A, dv=V_DIM, hb=H_A, tq=self.seq)
        rows = lambda a: a.reshape(self.batch, self.seq * H_B, HD_B)
        o_b = _attention(sh(q_b), rows(k_b), rows(v_b), None, None, heads=H_B, dv=HD_B, hb=H_B,
                         tq=self.seq, head_rows=True)
        return o_a, o_b


class _SampleAttend:
    cache_layout = False

    def __init__(self, k_ctx_a, v_ctx_a, k_ctx_b, v_ctx_b, bias):
        self.ctx = (k_ctx_a, v_ctx_a, k_ctx_b, v_ctx_b, bias)

    def __call__(self, q_a, k_a, v_a, q_b, k_b, v_b):
        k_ctx_a, v_ctx_a, k_ctx_b, v_ctx_b, bias = self.ctx
        o_a = _attention(q_a, k_a, v_a, k_ctx_a, v_ctx_a, heads=H_A, dv=V_DIM, hb=1, tq=MLA_TQ, tk=MLA_TK)
        o_b = _na_attention(q_b, k_b, v_b, k_ctx_b, v_ctx_b, bias)
        return o_a, o_b


def kernel(x_prompt, x_sample, cache_mla_ckv, cache_mla_krope, cache_na_k, cache_na_v, c, c_ctx,
           w_mod, b_mod, norm1_g, w_in, q_norm_g, kv_norm_g, w_uq, w_uk, w_uv, rpb,
           w_oa, w_ob, w_out, norm2_g, w_gu, w_down, norm_f_g):
    batch, seq, d = x_prompt.shape
    dec_batch, dec_seq, _ = x_sample.shape
    depth = w_mod.shape[0]
    assert depth == 1, "one trunk layer: the final norm is fused into the layer's last kernel"
    past = cache_mla_ckv.shape[2]
    q_lora, kv_lora = q_norm_g.shape[-1], kv_norm_g.shape[-1]
    na_w = H_B * HD_B
    rows = dec_seq // GRID_W
    assert rows % NA_QROWS == 0 and rows >= NA_KROWS

    n_cond = 1 + dec_batch
    r8 = -(-n_cond // 8) * 8
    cond = jnp.zeros((r8, d), F32).at[0].set(c_ctx).at[1:n_cond].set(c)

    l = 0
    mod3 = _adaln(cond, w_mod[l], b_mod[l]).reshape(r8, 1, 6 * d)

    wi = w_in[l]
    o = 0
    sec = {}
    for name, width in (("lat", q_lora + kv_lora + ROPE_DIM), ("q", na_w), ("k", na_w), ("v", na_w),
                        ("ga", d), ("gb", d)):
        sec[name] = wi[:, o:o + width]
        o += width
    lat_w = q_lora + kv_lora + LANES
    sec["lat"] = jnp.pad(sec["lat"], ((0, 0), (0, lat_w - sec["lat"].shape[1])))
    pieces, views, o = [], {}, 0
    for name in sorted(sec, key=lambda s: -sec[s].shape[1]):
        width = sec[name].shape[1]
        start = -(-o // width) * width
        if start > o:
            pieces.append(jnp.zeros((d, start - o), F32))
        pieces.append(sec[name])
        views[name] = (width, start // width)
        o = start + width
    w_in_packed = jnp.concatenate(pieces, axis=1).astype(BF16)
    view = lambda name: (w_in_packed,) + views[name]
    w_uq_p = jnp.pad(w_uq[l].reshape(q_lora, H_A, NOPE_DIM + ROPE_DIM),
                     ((0, 0), (0, 0), (0, QK_SLOT - NOPE_DIM - ROPE_DIM))).reshape(q_lora, H_A * QK_SLOT)
    w = {
        "w_lat": view("lat"), "w_q": view("q"), "w_k": view("k"), "w_v": view("v"),
        "w_ga": view("ga"), "w_gb": view("gb"),
        "w_uq_p": w_uq_p.astype(BF16), "w_uk": w_uk[l].astype(BF16), "w_uv": w_uv[l].astype(BF16),
        "w_oa": w_oa[l].astype(BF16), "w_ob": w_ob[l].astype(BF16), "w_out": w_out[l].astype(BF16),
        "w_gu": w_gu[l].astype(BF16), "w_down": w_down[l].astype(BF16),
        "norm1_g": norm1_g[l], "norm2_g": norm2_g[l], "q_norm_g": q_norm_g[l], "kv_norm_g": kv_norm_g[l],
        "norm_f_g": norm_f_g,
    }

    xp = x_prompt.reshape(1, batch * seq, d)
    yp, ckv_p, kr_p, k_b_p, v_b_p = _layer(xp, mod3, 0, w, None, _PromptAttend(batch, seq), seq)

    kr_ctx = jnp.pad(cache_mla_krope[:, l], ((0, 0), (0, 0), (0, LANES - ROPE_DIM)))
    k_ctx_a, v_ctx_a = _kvexp(cache_mla_ckv[:, l], kr_ctx, w["w_uk"], w["w_uv"])
    attend = _SampleAttend(k_ctx_a, v_ctx_a,
                           cache_na_k[:, l].reshape(dec_batch, past * H_B, HD_B),
                           cache_na_v[:, l].reshape(dec_batch, past * H_B, HD_B),
                           _na_bias_tables(rpb[l], rows))
    ys, _, _, _, _ = _layer(x_sample, mod3, 1, w, _rope_tables(dec_seq), attend, dec_seq)

    return (yp, ys,
            ckv_p.reshape(batch, 1, seq, kv_lora),
            kr_p.reshape(batch, seq, LANES)[:, :, :ROPE_DIM].reshape(batch, 1, seq, ROPE_DIM),
            k_b_p.reshape(batch, 1, seq, H_B, HD_B),
            v_b_p.reshape(batch, 1, seq, H_B, HD_B))
```

```python
import functools

import jax
import jax.numpy as jnp
import numpy as np
from jax import lax
from jax.experimental import pallas as pl
from jax.experimental.pallas import tpu as pltpu

F32 = jnp.float32
BF16 = jnp.bfloat16

GRID_W = 64
H_A = 8
NOPE_DIM = 128
ROPE_DIM = 64
V_DIM = 128
H_B = 8
HD_B = 128
WIN_R_MAX = 8
WIN_C = 16
ROPE_THETA = 10000.0
NORM_EPS = 1e-6
NEG_INF = -1e30
LOG2E = 1.4426950408889634
MLA_QSCALE = (NOPE_DIM + ROPE_DIM) ** -0.5 * LOG2E
NA_QSCALE = HD_B ** -0.5 * LOG2E

LANES = 128
MXU_COLS = 256
QK_SLOT = 256
NA_QROWS = 4
NA_KROWS = NA_QROWS + WIN_R_MAX
VMEM_LIMIT = 56 * 2 ** 20

ROW_TILE = 512
FFN_UP_ROWS = 1024
COL_TILE = 512
ADALN_COLS = 1024
PACK_ROWS = 256
MLA_TQ, MLA_TK = 1024, 512
ATTN_TK = 1024


def _cp(*sem):
    return pltpu.CompilerParams(dimension_semantics=sem, vmem_limit_bytes=VMEM_LIMIT)


def _dot(a, b):
    return jnp.dot(a, b, preferred_element_type=F32)


def _dot_nt(a, b):
    return lax.dot_general(a, b, (((1,), (1,)), ((), ())), preferred_element_type=F32)


def _sigmoid(x):
    return 1.0 / (1.0 + jnp.exp(-x))


def _rms(x, g):
    return x * lax.rsqrt(jnp.mean(x * x, axis=-1, keepdims=True) + NORM_EPS) * g


def _tile(n, want):
    t = min(n, want)
    while n % t:
        t //= 2
    return t


def _adaln_kernel(c_ref, w_ref, b_ref, o_ref):
    c = c_ref[...]
    s = (c * _sigmoid(c)).astype(BF16)
    o_ref[...] = _dot(s, w_ref[...].astype(BF16)) + b_ref[...]


def _adaln(cond, w_mod, b_mod):
    r, d = cond.shape
    n = w_mod.shape[1]
    tn = _tile(n, ADALN_COLS)
    return pl.pallas_call(
        _adaln_kernel,
        grid=(n // tn,),
        in_specs=[pl.BlockSpec((r, d), lambda j: (0, 0)),
                  pl.BlockSpec((d, tn), lambda j: (0, j)),
                  pl.BlockSpec((1, tn), lambda j: (0, j))],
        out_specs=pl.BlockSpec((r, tn), lambda j: (0, j)),
        out_shape=jax.ShapeDtypeStruct((r, n), F32),
        compiler_params=_cp("arbitrary"),
    )(cond, w_mod, b_mod.reshape(1, n))


def _col_view_spec(view, grid_rank):
    arr, width, idx = view
    index_map = (lambda i: (0, idx)) if grid_rank == 1 else (lambda b, i: (0, idx))
    return pl.BlockSpec((arr.shape[0], width), index_map, pipeline_mode=pl.Buffered(1))


def _mod_spec(d, sec, row0, grid_rank):
    if grid_rank == 2:
        return pl.BlockSpec((None, 1, d), lambda b, i: (row0 + b, 0, sec))
    return pl.BlockSpec((None, 1, d), lambda b, i, j: (row0 + b, 0, sec))


def _qkv_kernel(x_ref, g_ref, sh_ref, sc_ref, wq_ref, wk_ref, wv_ref,
                h_ref, q_ref, k_ref, v_ref, *, row_split, cache_layout):
    rs = x_ref.shape[0] // row_split
    for r in range(row_split):
        rows = slice(r * rs, (r + 1) * rs)
        h = (_rms(x_ref[rows, :], g_ref[...]) * (1.0 + sc_ref[...]) + sh_ref[...]).astype(BF16)
        h_ref[rows, :] = h
        q_ref[rows, :] = (_dot(h, wq_ref[...]) * NA_QSCALE).astype(BF16)
        k = _dot(h, wk_ref[...])
        v = _dot(h, wv_ref[...])
        if cache_layout:
            for hh in range(H_B):
                head_rows = pl.ds(r * rs * H_B + hh, rs, stride=H_B)
                k_ref[head_rows, :] = k[:, hh * HD_B:(hh + 1) * HD_B]
                v_ref[head_rows, :] = v[:, hh * HD_B:(hh + 1) * HD_B]
        else:
            k_ref[rows, :] = k.astype(BF16)
            one = jnp.ones((rs, HD_B), BF16)
            for hh in range(H_B):
                v_ref[rows, 2 * hh * HD_B:(2 * hh + 1) * HD_B] = v[:, hh * HD_B:(hh + 1) * HD_B].astype(BF16)
                v_ref[rows, (2 * hh + 1) * HD_B:(2 * hh + 2) * HD_B] = one


def _qkv(x, g, mod3, row0, wq, wk, wv, cache_layout):
    bm, sm, d = x.shape
    n = wq[1]
    ts = _tile(sm, ROW_TILE)
    tok = lambda w: pl.BlockSpec((None, ts, w), lambda b, i: (b, i, 0))
    if cache_layout:
        kv_specs = [pl.BlockSpec((None, ts * H_B, HD_B), lambda b, i: (b, i, 0))] * 2
        kv_shapes = [jax.ShapeDtypeStruct((bm, sm * H_B, HD_B), F32)] * 2
    else:
        kv_specs = [tok(n), tok(2 * n)]
        kv_shapes = [jax.ShapeDtypeStruct((bm, sm, n), BF16), jax.ShapeDtypeStruct((bm, sm, 2 * n), BF16)]
    return pl.pallas_call(
        functools.partial(_qkv_kernel, row_split=2 if ts % 16 == 0 else 1, cache_layout=cache_layout),
        grid=(bm, sm // ts),
        in_specs=[tok(d), pl.BlockSpec((1, d), lambda b, i: (0, 0)),
                  _mod_spec(d, 0, row0, 2), _mod_spec(d, 1, row0, 2)]
                 + [_col_view_spec(v, 2) for v in (wq, wk, wv)],
        out_specs=[tok(d), tok(n)] + kv_specs,
        out_shape=[jax.ShapeDtypeStruct((bm, sm, d), BF16), jax.ShapeDtypeStruct((bm, sm, n), BF16)] + kv_shapes,
        compiler_params=_cp("parallel", "parallel"),
    )(x, g.reshape(1, d), mod3, mod3, wq[0], wk[0], wv[0])


def _rope128(g, c, s1, s2):
    return g * c + pltpu.roll(g, 96, 1) * s1 + pltpu.roll(g, 32, 1) * s2


def _expand_kv(ckv, kr, wuk_ref, wuv_ref, k_ref, v_ref):
    c = ckv.astype(BF16)
    kn = _dot(c, wuk_ref[...])
    v = _dot(c, wuv_ref[...])
    krb = kr.astype(BF16)
    one = jnp.ones((c.shape[0], V_DIM), BF16)
    for hh in range(H_A):
        lo = hh * QK_SLOT
        k_ref[:, lo:lo + LANES] = kn[:, hh * NOPE_DIM:(hh + 1) * NOPE_DIM].astype(BF16)
        k_ref[:, lo + LANES:lo + QK_SLOT] = krb
        v_ref[:, 2 * hh * V_DIM:(2 * hh + 1) * V_DIM] = v[:, hh * V_DIM:(hh + 1) * V_DIM].astype(BF16)
        v_ref[:, (2 * hh + 1) * V_DIM:(2 * hh + 2) * V_DIM] = one


def _latent_kernel(*refs, q_lora, kv_lora, rope):
    if rope:
        (h_ref, wl_ref, qg_ref, kg_ref, wuq_ref, wuk_ref, wuv_ref, c_ref, s1_ref, s2_ref,
         q_ref, ckv_ref, kr_ref, k_ref, v_ref) = refs
    else:
        (h_ref, wl_ref, qg_ref, kg_ref, wuq_ref, wuk_ref, wuv_ref,
         q_ref, ckv_ref, kr_ref, k_ref, v_ref) = refs
    lat = _dot(h_ref[...], wl_ref[...])
    cq = lat[:, :q_lora]
    ckv = _rms(lat[:, q_lora:q_lora + kv_lora], kg_ref[...])
    kr = lat[:, q_lora + kv_lora:]
    ckv_ref[...] = ckv
    if rope:
        c, s1, s2 = c_ref[...], s1_ref[...], s2_ref[...]
        kr = _rope128(kr, c, s1, s2)
    kr_ref[...] = kr
    _expand_kv(ckv, kr, wuk_ref, wuv_ref, k_ref, v_ref)
    qa = _dot(_rms(cq, qg_ref[...]).astype(BF16), wuq_ref[...])
    for hh in range(H_A):
        lo = hh * QK_SLOT
        q_ref[:, lo:lo + LANES] = (qa[:, lo:lo + LANES] * MLA_QSCALE).astype(BF16)
        g = qa[:, lo + LANES:lo + QK_SLOT]
        if rope:
            g = _rope128(g, c, s1, s2)
        q_ref[:, lo + LANES:lo + QK_SLOT] = (g * MLA_QSCALE).astype(BF16)


def _latent(h, w_lat, q_norm_g, kv_norm_g, w_uq_p, w_uk, w_uv, rope_tabs):
    bm, sm, d = h.shape
    q_lora, kv_lora = q_norm_g.shape[-1], kv_norm_g.shape[-1]
    ts = _tile(sm, ROW_TILE)
    rope = rope_tabs is not None
    tok = lambda w: pl.BlockSpec((None, ts, w), lambda b, i: (b, i, 0))
    full = lambda a: pl.BlockSpec(a.shape, lambda b, i: (0, 0))
    qg, kg = q_norm_g.reshape(1, q_lora), kv_norm_g.reshape(1, kv_lora)
    args = [h, w_lat[0], qg, kg, w_uq_p, w_uk, w_uv]
    in_specs = [tok(d), _col_view_spec(w_lat, 2)] + [full(a) for a in args[2:]]
    if rope:
        in_specs += [pl.BlockSpec((ts, LANES), lambda b, i: (i, 0))] * 3
        args += list(rope_tabs)
    widths = (H_A * QK_SLOT, kv_lora, LANES, H_A * QK_SLOT, H_A * 2 * V_DIM)
    dtypes = (BF16, F32, F32, BF16, BF16)
    return pl.pallas_call(
        functools.partial(_latent_kernel, q_lora=q_lora, kv_lora=kv_lora, rope=rope),
        grid=(bm, sm // ts),
        in_specs=in_specs,
        out_specs=[tok(w) for w in widths],
        out_shape=[jax.ShapeDtypeStruct((bm, sm, w), t) for w, t in zip(widths, dtypes)],
        compiler_params=_cp("parallel", "parallel"),
    )(*args)


def _kvexp_kernel(ckv_ref, kr_ref, wuk_ref, wuv_ref, k_ref, v_ref):
    _expand_kv(ckv_ref[...], kr_ref[...], wuk_ref, wuv_ref, k_ref, v_ref)


def _kvexp(ckv, kr128, w_uk, w_uv):
    bm, sm, kv_lora = ckv.shape
    ts = _tile(sm, ROW_TILE)
    tok = lambda w: pl.BlockSpec((None, ts, w), lambda b, i: (b, i, 0))
    full = lambda a: pl.BlockSpec(a.shape, lambda b, i: (0, 0))
    widths = (H_A * QK_SLOT, H_A * 2 * V_DIM)
    return pl.pallas_call(
        _kvexp_kernel,
        grid=(bm, sm // ts),
        in_specs=[tok(kv_lora), tok(LANES), full(w_uk), full(w_uv)],
        out_specs=[tok(w) for w in widths],
        out_shape=[jax.ShapeDtypeStruct((bm, sm, w), BF16) for w in widths],
        compiler_params=_cp("parallel", "parallel"),
    )(ckv, kr128, w_uk, w_uv)


def _attn_kernel(*refs, hb, dk, dv, vw, two, tk, head_rows):
    ones = vw == 2 * dv
    if two:
        q_ref, k1_ref, v1_ref, k2_ref, v2_ref, o_ref = refs
    else:
        q_ref, k1_ref, v1_ref, o_ref = refs
    n_keys = k1_ref.shape[0] // max(head_rows, 1)
    chunks = [(k1_ref, v1_ref, c * tk, tk) for c in range(n_keys // tk)]
    if two:
        chunks.append((k2_ref, v2_ref, 0, k2_ref.shape[0]))

    def head_slab(ref, lo, n, j, w):
        if head_rows:
            return ref[pl.ds(lo * head_rows + j, n, stride=head_rows), :]
        return ref[lo:lo + n, j * w:(j + 1) * w]

    for j in range(hb):
        q = q_ref[:, j * dk:(j + 1) * dk]

        def logits(ch):
            k_ref, _, lo, n = ch
            return _dot_nt(q, head_slab(k_ref, lo, n, j, dk).astype(BF16))

        s_next = logits(chunks[0])
        m = l = acc = None
        for i, (_, v_ref, lo, n) in enumerate(chunks):
            s = s_next
            if i + 1 < len(chunks):
                s_next = logits(chunks[i + 1])
            mc = jnp.max(s, axis=-1, keepdims=True)
            m_new = mc if m is None else jnp.maximum(m, mc)
            p = jnp.exp2(s - m_new)
            pv = _dot(p.astype(BF16), head_slab(v_ref, lo, n, j, vw).astype(BF16))
            if not ones:
                ps = jnp.sum(p, axis=-1, keepdims=True)
            if m is None:
                acc = pv
                l = None if ones else ps
            else:
                alpha = jnp.exp2(m - m_new)
                acc = alpha * acc + pv
                l = None if ones else alpha * l + ps
            m = m_new
        if ones:
            o = acc[:, :dv] * (1.0 / acc[:, dv:])
        else:
            o = acc * (1.0 / l)
        o_ref[:, j * dv:(j + 1) * dv] = o.astype(o_ref.dtype)


def _attention(q, k1, v1, k2, v2, *, heads, dv, hb, tq, tk=ATTN_TK, head_rows=False):
    b, sq, qw = q.shape
    dk = qw // heads
    two = k2 is not None
    tq = _tile(sq, tq)
    if head_rows:
        assert hb == heads and not two
        vw, n_keys = v1.shape[-1], k1.shape[1] // heads
        kv = lambda a, w: pl.BlockSpec((None, a.shape[1], w), lambda bi, hg, qi: (bi, 0, 0))
    else:
        vw, n_keys = v1.shape[-1] // heads, k1.shape[1]
        kv = lambda a, w: pl.BlockSpec((None, a.shape[1], hb * w), lambda bi, hg, qi: (bi, 0, hg))
    in_specs = [pl.BlockSpec((None, tq, hb * dk), lambda bi, hg, qi: (bi, qi, hg)),
                kv(k1, dk), kv(v1, vw)]
    args = [q, k1, v1]
    if two:
        in_specs += [kv(k2, dk), kv(v2, vw)]
        args += [k2, v2]
    return pl.pallas_call(
        functools.partial(_attn_kernel, hb=hb, dk=dk, dv=dv, vw=vw, two=two,
                          tk=_tile(n_keys, tk), head_rows=heads if head_rows else 0),
        grid=(b, heads // hb, sq // tq),
        in_specs=in_specs,
        out_specs=pl.BlockSpec((None, tq, hb * dv), lambda bi, hg, qi: (bi, qi, hg)),
        out_shape=jax.ShapeDtypeStruct((b, sq, heads * dv), BF16),
        compiler_params=_cp("parallel", "parallel", "arbitrary"),
    )(*args)


def _na_bias_tables(rpb, rows):
    win_r = min(WIN_R_MAX, rows)
    qc = np.arange(GRID_W)[:, None]
    kc = np.arange(GRID_W)[None, :]
    qs = np.clip(qc - WIN_C // 2, 0, GRID_W - WIN_C)
    col_valid = (kc >= qs) & (kc < qs + WIN_C)
    col_off = np.clip(kc - qc + (WIN_C - 1), 0, 2 * WIN_C - 2)
    onehot = (col_off[None] == np.arange(2 * WIN_C - 1)[:, None, None]).astype(np.float32)
    e = jnp.einsum("hrc,cqk->hrqk", rpb, onehot, precision=lax.Precision.HIGHEST)
    e = jnp.where(col_valid, e * LOG2E, NEG_INF)
    n_roff = 2 * WIN_R_MAX - 1
    e = jnp.concatenate([e, jnp.full((H_B, 1, GRID_W, GRID_W), NEG_INF, F32)], axis=1)
    slab = []
    for r_blk in (0, NA_QROWS, rows - NA_QROWS):
        kr0 = int(np.clip(r_blk - win_r // 2, 0, rows - NA_KROWS))
        per_row = []
        for a in range(NA_QROWS):
            r = r_blk + a
            r0 = int(np.clip(r - win_r // 2, 0, rows - win_r))
            per_row.append([kr - r + (WIN_R_MAX - 1) if r0 <= kr < r0 + win_r else n_roff
                            for kr in range(kr0, kr0 + NA_KROWS)])
        slab.append(per_row)

    def assemble(e_ref, o_ref):
        for v, per_row in enumerate(slab):
            for a, idx in enumerate(per_row):
                o_ref[v, a * GRID_W:(a + 1) * GRID_W, :] = jnp.concatenate([e_ref[j] for j in idx], axis=-1)

    return pl.pallas_call(
        assemble,
        grid=(H_B,),
        in_specs=[pl.BlockSpec((None, n_roff + 1, GRID_W, GRID_W), lambda h: (h, 0, 0, 0))],
        out_specs=pl.BlockSpec((None, 3, NA_QROWS * GRID_W, NA_KROWS * GRID_W), lambda h: (h, 0, 0, 0)),
        out_shape=jax.ShapeDtypeStruct((H_B, 3, NA_QROWS * GRID_W, NA_KROWS * GRID_W), F32),
        compiler_params=_cp("parallel"),
    )(e)


def _na_kernel(q_ref, k_ref, v_ref, kc_ref, vc_ref, bias_ref, o_ref, *, rows):
    nblk = rows // NA_QROWS
    nq, nk = NA_QROWS * GRID_W, NA_KROWS * GRID_W
    n_ctx = kc_ref.shape[0] // H_B
    head_rows = pl.ds(pl.program_id(1), n_ctx, stride=H_B)
    kc = kc_ref[head_rows, :].astype(BF16)
    vc = jnp.concatenate([vc_ref[head_rows, :].astype(BF16), jnp.ones((n_ctx, HD_B), BF16)], axis=-1)

    for t in range(nblk):
        r_blk = t * NA_QROWS
        kr0 = int(np.clip(r_blk - min(WIN_R_MAX, rows) // 2, 0, rows - NA_KROWS))
        var = 0 if t == 0 else (2 if t == nblk - 1 else 1)
        q0, k0 = r_blk * GRID_W, kr0 * GRID_W
        q = q_ref[q0:q0 + nq, :]
        s = _dot_nt(q, k_ref[k0:k0 + nk, :]) + bias_ref[var]
        sc = _dot_nt(q, kc)
        m = jnp.maximum(jnp.max(s, axis=-1, keepdims=True), jnp.max(sc, axis=-1, keepdims=True))
        p = jnp.exp2(s - m)
        pc = jnp.exp2(sc - m)
        o = _dot(p.astype(BF16), v_ref[k0:k0 + nk, :]) + _dot(pc.astype(BF16), vc)
        o_ref[q0:q0 + nq, :] = (o[:, :HD_B] * (1.0 / o[:, HD_B:])).astype(o_ref.dtype)


def _na_attention(q, k, v, k_ctx, v_ctx, bias):
    b, s, _ = q.shape
    rows = s // GRID_W
    head = lambda n: pl.BlockSpec((None, n, HD_B), lambda bi, h: (bi, 0, h))
    ctx = pl.BlockSpec((None,) + k_ctx.shape[1:], lambda bi, h: (bi, 0, 0))
    return pl.pallas_call(
        functools.partial(_na_kernel, rows=rows),
        grid=(b, H_B),
        in_specs=[head(s), head(s),
                  pl.BlockSpec((None, s, 2 * HD_B), lambda bi, h: (bi, 0, h)),
                  ctx, ctx,
                  pl.BlockSpec((None,) + bias.shape[1:], lambda bi, h: (h, 0, 0, 0))],
        out_specs=head(s),
        out_shape=jax.ShapeDtypeStruct((b, s, H_B * HD_B), BF16),
        compiler_params=_cp("parallel", "arbitrary"),
    )(q, k, v, k_ctx, v_ctx, bias)


def _merge_kernel(oa_ref, ob_ref, h_ref, woa_ref, wob_ref, wga_ref, wgb_ref, m_ref, *, row_split):
    rs = h_ref.shape[0] // row_split
    for r in range(row_split):
        rows = slice(r * rs, (r + 1) * rs)
        h = h_ref[rows, :]
        ya = _dot(oa_ref[rows, :], woa_ref[...])
        yb = _dot(ob_ref[rows, :], wob_ref[...])
        ga = _sigmoid(_dot(h, wga_ref[...]))
        gb = _sigmoid(_dot(h, wgb_ref[...]))
        m_ref[rows, :] = (ga * ya + gb * yb).astype(BF16)


def _merge(oa, ob, h, w_oa, w_ob, w_ga, w_gb):
    m, d = h.shape
    n = w_oa.shape[1]
    tm = _tile(m, ROW_TILE)
    row = lambda a: pl.BlockSpec((tm, a.shape[1]), lambda i: (i, 0))
    full = lambda a: pl.BlockSpec(a.shape, lambda i: (0, 0), pipeline_mode=pl.Buffered(1))
    return pl.pallas_call(
        functools.partial(_merge_kernel, row_split=2 if tm % 16 == 0 else 1),
        grid=(m // tm,),
        in_specs=[row(oa), row(ob), row(h), full(w_oa), full(w_ob),
                  _col_view_spec(w_ga, 1), _col_view_spec(w_gb, 1)],
        out_specs=pl.BlockSpec((tm, n), lambda i: (i, 0)),
        out_shape=jax.ShapeDtypeStruct((m, n), BF16),
        compiler_params=_cp("parallel"),
    )(oa, ob, h, w_oa, w_ob, w_ga[0], w_gb[0])


def _mixout_kernel(m_ref, x_ref, w_ref, gt_ref, g_ref, sh_ref, sc_ref, x1_ref, h2_ref, *, row_split):
    rs = m_ref.shape[0] // row_split
    for r in range(row_split):
        rows = slice(r * rs, (r + 1) * rs)
        x1 = x_ref[rows, :] + gt_ref[...] * _dot(m_ref[rows, :], w_ref[...])
        x1_ref[rows, :] = x1
        h2_ref[rows, :] = (_rms(x1, g_ref[...]) * (1.0 + sc_ref[...]) + sh_ref[...]).astype(BF16)


def _mixout(mm, x, w_out, g2, mod3, row0):
    bm, sm, d = x.shape
    ts = _tile(sm, ROW_TILE)
    tok = pl.BlockSpec((None, ts, d), lambda b, i: (b, i, 0))
    return pl.pallas_call(
        functools.partial(_mixout_kernel, row_split=2 if ts % 16 == 0 else 1),
        grid=(bm, sm // ts),
        in_specs=[tok, tok, pl.BlockSpec((d, d), lambda b, i: (0, 0), pipeline_mode=pl.Buffered(1)),
                  _mod_spec(d, 2, row0, 2), pl.BlockSpec((1, d), lambda b, i: (0, 0)),
                  _mod_spec(d, 3, row0, 2), _mod_spec(d, 4, row0, 2)],
        out_specs=[tok, tok],
        out_shape=[jax.ShapeDtypeStruct((bm, sm, d), F32),
                   jax.ShapeDtypeStruct((bm, sm, d), BF16)],
        compiler_params=_cp("parallel", "parallel"),
    )(mm, x, w_out, mod3, g2.reshape(1, d), mod3, mod3)


def _ffn_up_kernel(h_ref, wg_ref, wu_ref, o_ref, *, row_split):
    rs = h_ref.shape[0] // row_split
    for r in range(row_split):
        h = h_ref[r * rs:(r + 1) * rs, :]
        g = _dot(h, wg_ref[...])
        u = _dot(h, wu_ref[...])
        o_ref[r * rs:(r + 1) * rs, :] = (g * _sigmoid(g) * u).astype(BF16)


def _ffn_up(h2, w_gu):
    m, d = h2.shape
    d_ff = w_gu.shape[1] // 2
    tm = _tile(m, FFN_UP_ROWS)
    tn = d_ff // 2 if d_ff % (2 * MXU_COLS) == 0 else _tile(d_ff, COL_TILE)
    nj = d_ff // tn
    wspec = lambda off: pl.BlockSpec((d, tn), lambda j, i: (0, j + off), pipeline_mode=pl.Buffered(1))
    return pl.pallas_call(
        functools.partial(_ffn_up_kernel, row_split=4 if tm % 32 == 0 else 1),
        grid=(nj, m // tm),
        in_specs=[pl.BlockSpec((tm, d), lambda j, i: (i, 0)), wspec(0), wspec(nj)],
        out_specs=pl.BlockSpec((tm, tn), lambda j, i: (i, j)),
        out_shape=jax.ShapeDtypeStruct((m, d_ff), BF16),
        compiler_params=_cp("arbitrary", "arbitrary"),
    )(h2, w_gu, w_gu)


def _ffn_down_kernel(a_ref, w_ref, x_hbm, gt_ref, g_ref, y_ref, x_buf, x_sem, *, row_split):
    b, i = pl.program_id(0), pl.program_id(1)
    ts, d = x_buf.shape
    x_copy = pltpu.make_async_copy(x_hbm.at[b, pl.ds(i * ts, ts), :], x_buf, x_sem)
    x_copy.start()
    rs = ts // row_split
    tn = _tile(d, COL_TILE)
    for r in range(row_split):
        rows = slice(r * rs, (r + 1) * rs)
        yr = y_ref.at[r] if len(y_ref.shape) == 3 else y_ref.at[pl.ds(r * rs, rs)]
        for n in range(d // tn):
            cols = slice(n * tn, (n + 1) * tn)
            yr[:, cols] = _dot(a_ref[rows, :], w_ref[:, cols])
        if r == 0:
            x_copy.wait()
        yr[...] = _rms(x_buf[rows, :] + gt_ref[...] * yr[...], g_ref[...])


def _ffn_down(hid, w_down, x1, norm_f_g, mod3, row0, out_seq):
    bm, sm, d = x1.shape
    d_ff = w_down.shape[0]
    ts = _tile(sm, ROW_TILE)
    if out_seq >= ts:
        assert out_seq == sm
        row_split = 2 if ts % 16 == 0 else 1
        out_spec = pl.BlockSpec((None, ts, d), lambda b, i: (b, i, 0))
    else:
        assert bm == 1 and ts % out_seq == 0
        row_split = ts // out_seq
        out_spec = pl.BlockSpec((row_split, out_seq, d), lambda b, i: (i, 0, 0))
    return pl.pallas_call(
        functools.partial(_ffn_down_kernel, row_split=row_split),
        grid=(bm, sm // ts),
        in_specs=[pl.BlockSpec((None, ts, d_ff), lambda b, i: (b, i, 0)),
                  pl.BlockSpec((d_ff, d), lambda b, i: (0, 0), pipeline_mode=pl.Buffered(1)),
                  pl.BlockSpec(memory_space=pl.ANY),
                  _mod_spec(d, 5, row0, 2),
                  pl.BlockSpec((1, d), lambda b, i: (0, 0))],
        out_specs=out_spec,
        out_shape=jax.ShapeDtypeStruct((bm * sm // out_seq, out_seq, d), F32),
        scratch_shapes=[pltpu.VMEM((ts, d), F32), pltpu.SemaphoreType.DMA(())],
        compiler_params=_cp("arbitrary", "arbitrary"),
    )(hid, w_down, x1, mod3, norm_f_g.reshape(1, d))


def _pack_kernel(w_ref, o_ref, *, moves):
    o_ref[...] = jnp.zeros(o_ref.shape, BF16)
    for src, width, dst in moves:
        o_ref[:, dst:dst + width] = w_ref[:, src:src + width].astype(BF16)


def _pack_columns(w, moves, out_cols):
    rows, cols = w.shape
    tr = _tile(rows, PACK_ROWS)
    return pl.pallas_call(
        functools.partial(_pack_kernel, moves=tuple(moves)),
        grid=(rows // tr,),
        in_specs=[pl.BlockSpec((tr, cols), lambda i: (i, 0))],
        out_specs=pl.BlockSpec((tr, out_cols), lambda i: (i, 0)),
        out_shape=jax.ShapeDtypeStruct((rows, out_cols), BF16),
        compiler_params=_cp("parallel"),
    )(w)


def _rope_tables(n_tokens):
    t = jnp.arange(n_tokens, dtype=jnp.int32)
    row = (t // GRID_W).astype(F32)
    col = (t % GRID_W).astype(F32)
    n_freq = ROPE_DIM // 4
    inv_freq = ROPE_THETA ** (-jnp.arange(n_freq, dtype=F32) / n_freq)
    ang = jnp.concatenate([row[:, None] * inv_freq, col[:, None] * inv_freq], axis=-1)
    cos, sin = jnp.cos(ang), jnp.sin(ang)
    z = jnp.zeros_like(cos)
    return (jnp.concatenate([cos, cos, z, z], axis=-1),
            jnp.concatenate([-sin, z, z, z], axis=-1),
            jnp.concatenate([z, sin, z, z], axis=-1))


def _layer(x, mod3, row0, w, rope_tabs, attend, out_seq):
    bm, sm, d = x.shape
    m = bm * sm
    h, q_b, k_b, v_b = _qkv(x, w["norm1_g"], mod3, row0, w["w_q"], w["w_k"], w["w_v"], attend.cache_layout)
    h2d = h.reshape(m, d)
    q_a, ckv, kr128, k_a, v_a = _latent(h, w["w_lat"], w["q_norm_g"], w["kv_norm_g"], w["w_uq_p"],
                                        w["w_uk"], w["w_uv"], rope_tabs)
    o_a, o_b = attend(q_a, k_a, v_a, q_b, k_b, v_b)
    mm = _merge(o_a.reshape(m, -1), o_b.reshape(m, -1), h2d, w["w_oa"], w["w_ob"], w["w_ga"], w["w_gb"])
    x1, hn = _mixout(mm.reshape(bm, sm, d), x, w["w_out"], w["norm2_g"], mod3, row0)
    hid = _ffn_up(hn.reshape(m, d), w["w_gu"])
    x2 = _ffn_down(hid.reshape(bm, sm, -1), w["w_down"], x1, w["norm_f_g"], mod3, row0, out_seq)
    return x2, ckv, kr128, k_b, v_b


class _PromptAttend:
    cache_layout = True

    def __init__(self, batch, seq):
        self.batch, self.seq = batch, seq

    def __call__(self, q_a, k_a, v_a, q_b, k_b, v_b):
        sh = lambda a: a.reshape(self.batch, self.seq, -1)
        o_a = _attention(sh(q_a), sh(k_a), sh(v_a), None, None, heads=H_A, dv=V_DIM, hb=H_A, tq=self.seq)
        rows = lambda a: a.reshape(self.batch, self.seq * H_B, HD_B)
        o_b = _attention(sh(q_b), rows(k_b), rows(v_b), None, None, heads=H_B, dv=HD_B, hb=H_B,
                         tq=self.seq, head_rows=True)
        return o_a, o_b


class _SampleAttend:
    cache_layout = False

    def __init__(self, k_ctx_a, v_ctx_a, k_ctx_b, v_ctx_b, bias):
        self.ctx = (k_ctx_a, v_ctx_a, k_ctx_b, v_ctx_b, bias)

    def __call__(self, q_a, k_a, v_a, q_b, k_b, v_b):
        k_ctx_a, v_ctx_a, k_ctx_b, v_ctx_b, bias = self.ctx
        o_a = _attention(q_a, k_a, v_a, k_ctx_a, v_ctx_a, heads=H_A, dv=V_DIM, hb=1, tq=MLA_TQ, tk=MLA_TK)
        o_b = _na_attention(q_b, k_b, v_b, k_ctx_b, v_ctx_b, bias)
        return o_a, o_b


def kernel(x_prompt, x_sample, cache_mla_ckv, cache_mla_krope, cache_na_k, cache_na_v, c, c_ctx,
           w_mod, b_mod, norm1_g, w_in, q_norm_g, kv_norm_g, w_uq, w_uk, w_uv, rpb,
           w_oa, w_ob, w_out, norm2_g, w_gu, w_down, norm_f_g):
    batch, seq, d = x_prompt.shape
    dec_batch, dec_seq, _ = x_sample.shape
    depth = w_mod.shape[0]
    assert depth == 1, "one trunk layer: the final norm is fused into the layer's last kernel"
    past = cache_mla_ckv.shape[2]
    q_lora, kv_lora = q_norm_g.shape[-1], kv_norm_g.shape[-1]
    na_w = H_B * HD_B
    rows = dec_seq // GRID_W
    assert rows % NA_QROWS == 0 and rows >= NA_KROWS

    n_cond = 1 + dec_batch
    r8 = -(-n_cond // 8) * 8
    cond = jnp.zeros((r8, d), F32).at[0].set(c_ctx).at[1:n_cond].set(c)

    l = 0
    mod3 = _adaln(cond, w_mod[l], b_mod[l]).reshape(r8, 1, 6 * d)

    src, o = {}, 0
    for name, width in (("lat", q_lora + kv_lora + ROPE_DIM), ("q", na_w), ("k", na_w), ("v", na_w),
                        ("ga", d), ("gb", d)):
        src[name] = (o, width)
        o += width
    slot = dict({name: width for name, (_, width) in src.items()}, lat=q_lora + kv_lora + LANES)
    moves, views, o = [], {}, 0
    for name in sorted(slot, key=lambda s: -slot[s]):
        start = -(-o // slot[name]) * slot[name]
        moves.append(src[name] + (start,))
        views[name] = (slot[name], start // slot[name])
        o = start + slot[name]
    w_in_packed = _pack_columns(w_in[l], moves, o)
    view = lambda name: (w_in_packed,) + views[name]
    w_uq_p = jnp.pad(w_uq[l].reshape(q_lora, H_A, NOPE_DIM + ROPE_DIM),
                     ((0, 0), (0, 0), (0, QK_SLOT - NOPE_DIM - ROPE_DIM))).reshape(q_lora, H_A * QK_SLOT)
    w = {
        "w_lat": view("lat"), "w_q": view("q"), "w_k": view("k"), "w_v": view("v"),
        "w_ga": view("ga"), "w_gb": view("gb"),
        "w_uq_p": w_uq_p.astype(BF16), "w_uk": w_uk[l].astype(BF16), "w_uv": w_uv[l].astype(BF16),
        "w_oa": w_oa[l].astype(BF16), "w_ob": w_ob[l].astype(BF16), "w_out": w_out[l].astype(BF16),
        "w_gu": w_gu[l].astype(BF16), "w_down": w_down[l].astype(BF16),
        "norm1_g": norm1_g[l], "norm2_g": norm2_g[l], "q_norm_g": q_norm_g[l], "kv_norm_g": kv_norm_g[l],
        "norm_f_g": norm_f_g,
    }

    xp = x_prompt.reshape(1, batch * seq, d)
    yp, ckv_p, kr_p, k_b_p, v_b_p = _layer(xp, mod3, 0, w, None, _PromptAttend(batch, seq), seq)

    kr_ctx = jnp.pad(cache_mla_krope[:, l], ((0, 0), (0, 0), (0, LANES - ROPE_DIM)))
    k_ctx_a, v_ctx_a = _kvexp(cache_mla_ckv[:, l], kr_ctx, w["w_uk"], w["w_uv"])
    attend = _SampleAttend(k_ctx_a, v_ctx_a,
                           cache_na_k[:, l].reshape(dec_batch, past * H_B, HD_B),
                           cache_na_v[:, l].reshape(dec_batch, past * H_B, HD_B),
                           _na_bias_tables(rpb[l], rows))
    ys, _, _, _, _ = _layer(x_sample, mod3, 1, w, _rope_tables(dec_seq), attend, dec_seq)

    return (yp, ys,
            ckv_p.reshape(batch, 1, seq, kv_lora),
            kr_p.reshape(batch, seq, LANES)[:, :, :ROPE_DIM].reshape(batch, 1, seq, ROPE_DIM),
            k_b_p.reshape(batch, 1, seq, H_B, HD_B),
            v_b_p.reshape(batch, 1, seq, H_B, HD_B))
```

```python
import functools

import jax
import jax.numpy as jnp
import numpy as np
from jax import lax
from jax.experimental import pallas as pl
from jax.experimental.pallas import tpu as pltpu

F32 = jnp.float32
BF16 = jnp.bfloat16

GRID_W = 64
H_A = 8
NOPE_DIM = 128
ROPE_DIM = 64
V_DIM = 128
H_B = 8
HD_B = 128
WIN_R_MAX = 8
WIN_C = 16
ROPE_THETA = 10000.0
NORM_EPS = 1e-6
NEG_INF = -1e30
LOG2E = 1.4426950408889634
MLA_QSCALE = (NOPE_DIM + ROPE_DIM) ** -0.5 * LOG2E
NA_QSCALE = HD_B ** -0.5 * LOG2E

LANES = 128
MXU_COLS = 256
QK_SLOT = 256
NA_QROWS = 4
NA_KROWS = NA_QROWS + WIN_R_MAX
VMEM_LIMIT = 56 * 2 ** 20

ROW_TILE = 512
FFN_UP_ROWS = 1024
COL_TILE = 512
ADALN_COLS = 1024
PACK_ROWS = 256
MLA_TQ, MLA_TK = 1024, 512
ATTN_TK = 1024


def _cp(*sem):
    return pltpu.CompilerParams(dimension_semantics=sem, vmem_limit_bytes=VMEM_LIMIT)


def _dot(a, b):
    return jnp.dot(a, b, preferred_element_type=F32)


def _dot_nt(a, b):
    return lax.dot_general(a, b, (((1,), (1,)), ((), ())), preferred_element_type=F32)


def _sigmoid(x):
    return 1.0 / (1.0 + jnp.exp(-x))


def _rms(x, g):
    return x * lax.rsqrt(jnp.mean(x * x, axis=-1, keepdims=True) + NORM_EPS) * g


def _tile(n, want):
    t = min(n, want)
    while n % t:
        t //= 2
    return t


def _adaln_kernel(c_ref, w_ref, b_ref, o_ref):
    c = c_ref[...]
    s = (c * _sigmoid(c)).astype(BF16)
    o_ref[...] = _dot(s, w_ref[...].astype(BF16)) + b_ref[...]


def _adaln(cond, w_mod, b_mod):
    r, d = cond.shape
    n = w_mod.shape[1]
    tn = _tile(n, ADALN_COLS)
    return pl.pallas_call(
        _adaln_kernel,
        grid=(n // tn,),
        in_specs=[pl.BlockSpec((r, d), lambda j: (0, 0)),
                  pl.BlockSpec((d, tn), lambda j: (0, j)),
                  pl.BlockSpec((1, tn), lambda j: (0, j))],
        out_specs=pl.BlockSpec((r, tn), lambda j: (0, j)),
        out_shape=jax.ShapeDtypeStruct((r, n), F32),
        compiler_params=_cp("arbitrary"),
    )(cond, w_mod, b_mod.reshape(1, n))


def _col_view_spec(view, grid_rank):
    arr, width, idx = view
    index_map = (lambda i: (0, idx)) if grid_rank == 1 else (lambda b, i: (0, idx))
    return pl.BlockSpec((arr.shape[0], width), index_map, pipeline_mode=pl.Buffered(1))


def _mod_spec(d, sec, row0, grid_rank):
    if grid_rank == 2:
        return pl.BlockSpec((None, 1, d), lambda b, i: (row0 + b, 0, sec))
    return pl.BlockSpec((None, 1, d), lambda b, i, j: (row0 + b, 0, sec))


def _qkv_kernel(x_ref, g_ref, sh_ref, sc_ref, wq_ref, wk_ref, wv_ref,
                h_ref, q_ref, k_ref, v_ref, *, row_split, cache_layout):
    rs = x_ref.shape[0] // row_split
    for r in range(row_split):
        rows = slice(r * rs, (r + 1) * rs)
        h = (_rms(x_ref[rows, :], g_ref[...]) * (1.0 + sc_ref[...]) + sh_ref[...]).astype(BF16)
        h_ref[rows, :] = h
        q_ref[rows, :] = (_dot(h, wq_ref[...]) * NA_QSCALE).astype(BF16)
        k = _dot(h, wk_ref[...])
        v = _dot(h, wv_ref[...])
        if cache_layout:
            for hh in range(H_B):
                head_rows = pl.ds(r * rs * H_B + hh, rs, stride=H_B)
                k_ref[head_rows, :] = k[:, hh * HD_B:(hh + 1) * HD_B]
                v_ref[head_rows, :] = v[:, hh * HD_B:(hh + 1) * HD_B]
        else:
            k_ref[rows, :] = k.astype(BF16)
            one = jnp.ones((rs, HD_B), BF16)
            for hh in range(H_B):
                v_ref[rows, 2 * hh * HD_B:(2 * hh + 1) * HD_B] = v[:, hh * HD_B:(hh + 1) * HD_B].astype(BF16)
                v_ref[rows, (2 * hh + 1) * HD_B:(2 * hh + 2) * HD_B] = one


def _qkv(x, g, mod3, row0, wq, wk, wv, cache_layout):
    bm, sm, d = x.shape
    n = wq[1]
    ts = _tile(sm, ROW_TILE)
    tok = lambda w: pl.BlockSpec((None, ts, w), lambda b, i: (b, i, 0))
    if cache_layout:
        kv_specs = [pl.BlockSpec((None, ts * H_B, HD_B), lambda b, i: (b, i, 0))] * 2
        kv_shapes = [jax.ShapeDtypeStruct((bm, sm * H_B, HD_B), F32)] * 2
    else:
        kv_specs = [tok(n), tok(2 * n)]
        kv_shapes = [jax.ShapeDtypeStruct((bm, sm, n), BF16), jax.ShapeDtypeStruct((bm, sm, 2 * n), BF16)]
    return pl.pallas_call(
        functools.partial(_qkv_kernel, row_split=2 if ts % 16 == 0 else 1, cache_layout=cache_layout),
        grid=(bm, sm // ts),
        in_specs=[tok(d), pl.BlockSpec((1, d), lambda b, i: (0, 0)),
                  _mod_spec(d, 0, row0, 2), _mod_spec(d, 1, row0, 2)]
                 + [_col_view_spec(v, 2) for v in (wq, wk, wv)],
        out_specs=[tok(d), tok(n)] + kv_specs,
        out_shape=[jax.ShapeDtypeStruct((bm, sm, d), BF16), jax.ShapeDtypeStruct((bm, sm, n), BF16)] + kv_shapes,
        compiler_params=_cp("parallel", "parallel"),
    )(x, g.reshape(1, d), mod3, mod3, wq[0], wk[0], wv[0])


def _rope128(g, c, s1, s2):
    return g * c + pltpu.roll(g, 96, 1) * s1 + pltpu.roll(g, 32, 1) * s2


def _expand_kv(ckv, kr, wuk_ref, wuv_ref, k_ref, v_ref):
    c = ckv.astype(BF16)
    kn = _dot(c, wuk_ref[...])
    v = _dot(c, wuv_ref[...])
    krb = kr.astype(BF16)
    one = jnp.ones((c.shape[0], V_DIM), BF16)
    for hh in range(H_A):
        lo = hh * QK_SLOT
        k_ref[:, lo:lo + LANES] = kn[:, hh * NOPE_DIM:(hh + 1) * NOPE_DIM].astype(BF16)
        k_ref[:, lo + LANES:lo + QK_SLOT] = krb
        v_ref[:, 2 * hh * V_DIM:(2 * hh + 1) * V_DIM] = v[:, hh * V_DIM:(hh + 1) * V_DIM].astype(BF16)
        v_ref[:, (2 * hh + 1) * V_DIM:(2 * hh + 2) * V_DIM] = one


def _latent_kernel(*refs, q_lora, kv_lora, rope):
    if rope:
        (h_ref, wl_ref, qg_ref, kg_ref, wuq_ref, wuk_ref, wuv_ref, c_ref, s1_ref, s2_ref,
         q_ref, ckv_ref, kr_ref, k_ref, v_ref) = refs
    else:
        (h_ref, wl_ref, qg_ref, kg_ref, wuq_ref, wuk_ref, wuv_ref,
         q_ref, ckv_ref, kr_ref, k_ref, v_ref) = refs
    lat = _dot(h_ref[...], wl_ref[...])
    cq = lat[:, :q_lora]
    ckv = _rms(lat[:, q_lora:q_lora + kv_lora], kg_ref[...])
    kr = lat[:, q_lora + kv_lora:]
    ckv_ref[...] = ckv
    if rope:
        c, s1, s2 = c_ref[...], s1_ref[...], s2_ref[...]
        kr = _rope128(kr, c, s1, s2)
    kr_ref[...] = kr
    _expand_kv(ckv, kr, wuk_ref, wuv_ref, k_ref, v_ref)
    qa = _dot(_rms(cq, qg_ref[...]).astype(BF16), wuq_ref[...])
    for hh in range(H_A):
        lo = hh * QK_SLOT
        q_ref[:, lo:lo + LANES] = (qa[:, lo:lo + LANES] * MLA_QSCALE).astype(BF16)
        g = qa[:, lo + LANES:lo + QK_SLOT]
        if rope:
            g = _rope128(g, c, s1, s2)
        q_ref[:, lo + LANES:lo + QK_SLOT] = (g * MLA_QSCALE).astype(BF16)


def _latent(h, w_lat, q_norm_g, kv_norm_g, w_uq_p, w_uk, w_uv, rope_tabs):
    bm, sm, d = h.shape
    q_lora, kv_lora = q_norm_g.shape[-1], kv_norm_g.shape[-1]
    ts = _tile(sm, ROW_TILE)
    rope = rope_tabs is not None
    tok = lambda w: pl.BlockSpec((None, ts, w), lambda b, i: (b, i, 0))
    full = lambda a: pl.BlockSpec(a.shape, lambda b, i: (0, 0))
    qg, kg = q_norm_g.reshape(1, q_lora), kv_norm_g.reshape(1, kv_lora)
    args = [h, w_lat[0], qg, kg, w_uq_p, w_uk, w_uv]
    in_specs = [tok(d), _col_view_spec(w_lat, 2)] + [full(a) for a in args[2:]]
    if rope:
        in_specs += [pl.BlockSpec((ts, LANES), lambda b, i: (i, 0))] * 3
        args += list(rope_tabs)
    widths = (H_A * QK_SLOT, kv_lora, LANES, H_A * QK_SLOT, H_A * 2 * V_DIM)
    dtypes = (BF16, F32, F32, BF16, BF16)
    return pl.pallas_call(
        functools.partial(_latent_kernel, q_lora=q_lora, kv_lora=kv_lora, rope=rope),
        grid=(bm, sm // ts),
        in_specs=in_specs,
        out_specs=[tok(w) for w in widths],
        out_shape=[jax.ShapeDtypeStruct((bm, sm, w), t) for w, t in zip(widths, dtypes)],
        compiler_params=_cp("parallel", "parallel"),
    )(*args)


def _kvexp_kernel(ckv_ref, kr_ref, wuk_ref, wuv_ref, k_ref, v_ref):
    _expand_kv(ckv_ref[...], kr_ref[...], wuk_ref, wuv_ref, k_ref, v_ref)


def _kvexp(ckv, kr128, w_uk, w_uv):
    bm, sm, kv_lora = ckv.shape
    ts = _tile(sm, ROW_TILE)
    tok = lambda w: pl.BlockSpec((None, ts, w), lambda b, i: (b, i, 0))
    full = lambda a: pl.BlockSpec(a.shape, lambda b, i: (0, 0))
    widths = (H_A * QK_SLOT, H_A * 2 * V_DIM)
    return pl.pallas_call(
        _kvexp_kernel,
        grid=(bm, sm // ts),
        in_specs=[tok(kv_lora), tok(LANES), full(w_uk), full(w_uv)],
        out_specs=[tok(w) for w in widths],
        out_shape=[jax.ShapeDtypeStruct((bm, sm, w), BF16) for w in widths],
        compiler_params=_cp("parallel", "parallel"),
    )(ckv, kr128, w_uk, w_uv)


def _attn_kernel(*refs, hb, dk, dv, vw, two, tk, head_rows):
    ones = vw == 2 * dv
    if two:
        q_ref, k1_ref, v1_ref, k2_ref, v2_ref, o_ref = refs
    else:
        q_ref, k1_ref, v1_ref, o_ref = refs
    n_keys = k1_ref.shape[0] // max(head_rows, 1)
    chunks = [(k1_ref, v1_ref, c * tk, tk) for c in range(n_keys // tk)]
    if two:
        chunks.append((k2_ref, v2_ref, 0, k2_ref.shape[0]))

    def head_slab(ref, lo, n, j, w):
        if head_rows:
            return ref[pl.ds(lo * head_rows + j, n, stride=head_rows), :]
        return ref[lo:lo + n, j * w:(j + 1) * w]

    for j in range(hb):
        q = q_ref[:, j * dk:(j + 1) * dk]

        def logits(ch):
            k_ref, _, lo, n = ch
            return _dot_nt(q, head_slab(k_ref, lo, n, j, dk).astype(BF16))

        s_next = logits(chunks[0])
        m = l = acc = None
        for i, (_, v_ref, lo, n) in enumerate(chunks):
            s = s_next
            if i + 1 < len(chunks):
                s_next = logits(chunks[i + 1])
            mc = jnp.max(s, axis=-1, keepdims=True)
            m_new = mc if m is None else jnp.maximum(m, mc)
            p = jnp.exp2(s - m_new)
            pv = _dot(p.astype(BF16), head_slab(v_ref, lo, n, j, vw).astype(BF16))
            if not ones:
                ps = jnp.sum(p, axis=-1, keepdims=True)
            if m is None:
                acc = pv
                l = None if ones else ps
            else:
                alpha = jnp.exp2(m - m_new)
                acc = alpha * acc + pv
                l = None if ones else alpha * l + ps
            m = m_new
        if ones:
            o = acc[:, :dv] * (1.0 / acc[:, dv:])
        else:
            o = acc * (1.0 / l)
        o_ref[:, j * dv:(j + 1) * dv] = o.astype(o_ref.dtype)


def _attention(q, k1, v1, k2, v2, *, heads, dv, hb, tq, tk=ATTN_TK, head_rows=False):
    b, sq, qw = q.shape
    dk = qw // heads
    two = k2 is not None
    tq = _tile(sq, tq)
    if head_rows:
        assert hb == heads and not two
        vw, n_keys = v1.shape[-1], k1.shape[1] // heads
        kv = lambda a, w: pl.BlockSpec((None, a.shape[1], w), lambda bi, hg, qi: (bi, 0, 0))
    else:
        vw, n_keys = v1.shape[-1] // heads, k1.shape[1]
        kv = lambda a, w: pl.BlockSpec((None, a.shape[1], hb * w), lambda bi, hg, qi: (bi, 0, hg))
    in_specs = [pl.BlockSpec((None, tq, hb * dk), lambda bi, hg, qi: (bi, qi, hg)),
                kv(k1, dk), kv(v1, vw)]
    args = [q, k1, v1]
    if two:
        in_specs += [kv(k2, dk), kv(v2, vw)]
        args += [k2, v2]
    return pl.pallas_call(
        functools.partial(_attn_kernel, hb=hb, dk=dk, dv=dv, vw=vw, two=two,
                          tk=_tile(n_keys, tk), head_rows=heads if head_rows else 0),
        grid=(b, heads // hb, sq // tq),
        in_specs=in_specs,
        out_specs=pl.BlockSpec((None, tq, hb * dv), lambda bi, hg, qi: (bi, qi, hg)),
        out_shape=jax.ShapeDtypeStruct((b, sq, heads * dv), BF16),
        compiler_params=_cp("parallel", "parallel", "arbitrary"),
    )(*args)


def _na_bias_tables(rpb, rows):
    win_r = min(WIN_R_MAX, rows)
    qc = np.arange(GRID_W)[:, None]
    kc = np.arange(GRID_W)[None, :]
    qs = np.clip(qc - WIN_C // 2, 0, GRID_W - WIN_C)
    col_valid = (kc >= qs) & (kc < qs + WIN_C)
    col_off = np.clip(kc - qc + (WIN_C - 1), 0, 2 * WIN_C - 2)
    onehot = (col_off[None] == np.arange(2 * WIN_C - 1)[:, None, None]).astype(np.float32)
    e = jnp.einsum("hrc,cqk->hrqk", rpb, onehot, precision=lax.Precision.HIGHEST)
    e = jnp.where(col_valid, e * LOG2E, NEG_INF)
    n_roff = 2 * WIN_R_MAX - 1
    e = jnp.concatenate([e, jnp.full((H_B, 1, GRID_W, GRID_W), NEG_INF, F32)], axis=1)
    slab = []
    for r_blk in (0, NA_QROWS, rows - NA_QROWS):
        kr0 = int(np.clip(r_blk - win_r // 2, 0, rows - NA_KROWS))
        per_row = []
        for a in range(NA_QROWS):
            r = r_blk + a
            r0 = int(np.clip(r - win_r // 2, 0, rows - win_r))
            per_row.append([kr - r + (WIN_R_MAX - 1) if r0 <= kr < r0 + win_r else n_roff
                            for kr in range(kr0, kr0 + NA_KROWS)])
        slab.append(per_row)

    def assemble(e_ref, o_ref):
        for v, per_row in enumerate(slab):
            for a, idx in enumerate(per_row):
                o_ref[v, a * GRID_W:(a + 1) * GRID_W, :] = jnp.concatenate([e_ref[j] for j in idx], axis=-1)

    return pl.pallas_call(
        assemble,
        grid=(H_B,),
        in_specs=[pl.BlockSpec((None, n_roff + 1, GRID_W, GRID_W), lambda h: (h, 0, 0, 0))],
        out_specs=pl.BlockSpec((None, 3, NA_QROWS * GRID_W, NA_KROWS * GRID_W), lambda h: (h, 0, 0, 0)),
        out_shape=jax.ShapeDtypeStruct((H_B, 3, NA_QROWS * GRID_W, NA_KROWS * GRID_W), F32),
        compiler_params=_cp("parallel"),
    )(e)


def _na_kernel(q_ref, k_ref, v_ref, kc_ref, vc_ref, bias_ref, o_ref, *, rows):
    nblk = rows // NA_QROWS
    nq, nk = NA_QROWS * GRID_W, NA_KROWS * GRID_W
    n_ctx = kc_ref.shape[0] // H_B
    head_rows = pl.ds(pl.program_id(1), n_ctx, stride=H_B)
    kc = kc_ref[head_rows, :].astype(BF16)
    vc = jnp.concatenate([vc_ref[head_rows, :].astype(BF16), jnp.ones((n_ctx, HD_B), BF16)], axis=-1)

    for t in range(nblk):
        r_blk = t * NA_QROWS
        kr0 = int(np.clip(r_blk - min(WIN_R_MAX, rows) // 2, 0, rows - NA_KROWS))
        var = 0 if t == 0 else (2 if t == nblk - 1 else 1)
        q0, k0 = r_blk * GRID_W, kr0 * GRID_W
        q = q_ref[q0:q0 + nq, :]
        s = _dot_nt(q, k_ref[k0:k0 + nk, :]) + bias_ref[var]
        sc = _dot_nt(q, kc)
        m = jnp.maximum(jnp.max(s, axis=-1, keepdims=True), jnp.max(sc, axis=-1, keepdims=True))
        p = jnp.exp2(s - m)
        pc = jnp.exp2(sc - m)
        o = _dot(p.astype(BF16), v_ref[k0:k0 + nk, :]) + _dot(pc.astype(BF16), vc)
        o_ref[q0:q0 + nq, :] = (o[:, :HD_B] * (1.0 / o[:, HD_B:])).astype(o_ref.dtype)


def _na_attention(q, k, v, k_ctx, v_ctx, bias):
    b, s, _ = q.shape
    rows = s // GRID_W
    head = lambda n: pl.BlockSpec((None, n, HD_B), lambda bi, h: (bi, 0, h))
    ctx = pl.BlockSpec((None,) + k_ctx.shape[1:], lambda bi, h: (bi, 0, 0))
    return pl.pallas_call(
        functools.partial(_na_kernel, rows=rows),
        grid=(b, H_B),
        in_specs=[head(s), head(s),
                  pl.BlockSpec((None, s, 2 * HD_B), lambda bi, h: (bi, 0, h)),
                  ctx, ctx,
                  pl.BlockSpec((None,) + bias.shape[1:], lambda bi, h: (h, 0, 0, 0))],
        out_specs=head(s),
        out_shape=jax.ShapeDtypeStruct((b, s, H_B * HD_B), BF16),
        compiler_params=_cp("parallel", "arbitrary"),
    )(q, k, v, k_ctx, v_ctx, bias)


def _merge_kernel(oa_ref, ob_ref, h_ref, woa_ref, wob_ref, wga_ref, wgb_ref, m_ref, *, row_split):
    rs = h_ref.shape[0] // row_split
    for r in range(row_split):
        rows = slice(r * rs, (r + 1) * rs)
        h = h_ref[rows, :]
        ya = _dot(oa_ref[rows, :], woa_ref[...])
        yb = _dot(ob_ref[rows, :], wob_ref[...])
        ga = _sigmoid(_dot(h, wga_ref[...]))
        gb = _sigmoid(_dot(h, wgb_ref[...]))
        m_ref[rows, :] = (ga * ya + gb * yb).astype(BF16)


def _merge(oa, ob, h, w_oa, w_ob, w_ga, w_gb):
    m, d = h.shape
    n = w_oa.shape[1]
    tm = _tile(m, ROW_TILE)
    row = lambda a: pl.BlockSpec((tm, a.shape[1]), lambda i: (i, 0))
    full = lambda a: pl.BlockSpec(a.shape, lambda i: (0, 0), pipeline_mode=pl.Buffered(1))
    return pl.pallas_call(
        functools.partial(_merge_kernel, row_split=2 if tm % 16 == 0 else 1),
        grid=(m // tm,),
        in_specs=[row(oa), row(ob), row(h), full(w_oa), full(w_ob),
                  _col_view_spec(w_ga, 1), _col_view_spec(w_gb, 1)],
        out_specs=pl.BlockSpec((tm, n), lambda i: (i, 0)),
        out_shape=jax.ShapeDtypeStruct((m, n), BF16),
        compiler_params=_cp("parallel"),
    )(oa, ob, h, w_oa, w_ob, w_ga[0], w_gb[0])


def _mixout_kernel(m_ref, x_ref, w_ref, gt_ref, g_ref, sh_ref, sc_ref, x1_ref, h2_ref, *, row_split):
    rs = m_ref.shape[0] // row_split
    for r in range(row_split):
        rows = slice(r * rs, (r + 1) * rs)
        x1 = x_ref[rows, :] + gt_ref[...] * _dot(m_ref[rows, :], w_ref[...])
        x1_ref[rows, :] = x1
        h2_ref[rows, :] = (_rms(x1, g_ref[...]) * (1.0 + sc_ref[...]) + sh_ref[...]).astype(BF16)


def _mixout(mm, x, w_out, g2, mod3, row0):
    bm, sm, d = x.shape
    ts = _tile(sm, ROW_TILE)
    tok = pl.BlockSpec((None, ts, d), lambda b, i: (b, i, 0))
    return pl.pallas_call(
        functools.partial(_mixout_kernel, row_split=2 if ts % 16 == 0 else 1),
        grid=(bm, sm // ts),
        in_specs=[tok, tok, pl.BlockSpec((d, d), lambda b, i: (0, 0), pipeline_mode=pl.Buffered(1)),
                  _mod_spec(d, 2, row0, 2), pl.BlockSpec((1, d), lambda b, i: (0, 0)),
                  _mod_spec(d, 3, row0, 2), _mod_spec(d, 4, row0, 2)],
        out_specs=[tok, tok],
        out_shape=[jax.ShapeDtypeStruct((bm, sm, d), F32),
                   jax.ShapeDtypeStruct((bm, sm, d), BF16)],
        compiler_params=_cp("parallel", "parallel"),
    )(mm, x, w_out, mod3, g2.reshape(1, d), mod3, mod3)


def _ffn_up_kernel(h_ref, wg_ref, wu_ref, o_ref, *, row_split):
    rs = h_ref.shape[0] // row_split
    for r in range(row_split):
        h = h_ref[r * rs:(r + 1) * rs, :]
        g = _dot(h, wg_ref[...])
        u = _dot(h, wu_ref[...])
        o_ref[r * rs:(r + 1) * rs, :] = (g * _sigmoid(g) * u).astype(BF16)


def _ffn_up(h2, w_gu):
    m, d = h2.shape
    d_ff = w_gu.shape[1] // 2
    tm = _tile(m, FFN_UP_ROWS)
    tn = d_ff // 2 if d_ff % (2 * MXU_COLS) == 0 else _tile(d_ff, COL_TILE)
    nj = d_ff // tn
    wspec = lambda off: pl.BlockSpec((d, tn), lambda j, i: (0, j + off), pipeline_mode=pl.Buffered(1))
    return pl.pallas_call(
        functools.partial(_ffn_up_kernel, row_split=4 if tm % 32 == 0 else 1),
        grid=(nj, m // tm),
        in_specs=[pl.BlockSpec((tm, d), lambda j, i: (i, 0)), wspec(0), wspec(nj)],
        out_specs=pl.BlockSpec((tm, tn), lambda j, i: (i, j)),
        out_shape=jax.ShapeDtypeStruct((m, d_ff), BF16),
        compiler_params=_cp("arbitrary", "arbitrary"),
    )(h2, w_gu, w_gu)


def _ffn_down_kernel(a_ref, w_ref, x_hbm, gt_ref, g_ref, y_ref, x_buf, x_sem, *, row_split):
    b, i = pl.program_id(0), pl.program_id(1)
    ts, d = x_buf.shape
    x_copy = pltpu.make_async_copy(x_hbm.at[b, pl.ds(i * ts, ts), :], x_buf, x_sem)
    x_copy.start()
    rs = ts // row_split
    tn = _tile(d, COL_TILE)
    for r in range(row_split):
        rows = slice(r * rs, (r + 1) * rs)
        yr = y_ref.at[r] if len(y_ref.shape) == 3 else y_ref.at[pl.ds(r * rs, rs)]
        for n in range(d // tn):
            cols = slice(n * tn, (n + 1) * tn)
            yr[:, cols] = _dot(a_ref[rows, :], w_ref[:, cols])
        if r == 0:
            x_copy.wait()
        yr[...] = _rms(x_buf[rows, :] + gt_ref[...] * yr[...], g_ref[...])


def _ffn_down(hid, w_down, x1, norm_f_g, mod3, row0, out_seq):
    bm, sm, d = x1.shape
    d_ff = w_down.shape[0]
    ts = _tile(sm, ROW_TILE)
    if out_seq >= ts:
        assert out_seq == sm
        row_split = 2 if ts % 16 == 0 else 1
        out_spec = pl.BlockSpec((None, ts, d), lambda b, i: (b, i, 0))
    else:
        assert bm == 1 and ts % out_seq == 0
        row_split = ts // out_seq
        out_spec = pl.BlockSpec((row_split, out_seq, d), lambda b, i: (i, 0, 0))
    return pl.pallas_call(
        functools.partial(_ffn_down_kernel, row_split=row_split),
        grid=(bm, sm // ts),
        in_specs=[pl.BlockSpec((None, ts, d_ff), lambda b, i: (b, i, 0)),
                  pl.BlockSpec((d_ff, d), lambda b, i: (0, 0), pipeline_mode=pl.Buffered(1)),
                  pl.BlockSpec(memory_space=pl.ANY),
                  _mod_spec(d, 5, row0, 2),
                  pl.BlockSpec((1, d), lambda b, i: (0, 0))],
        out_specs=out_spec,
        out_shape=jax.ShapeDtypeStruct((bm * sm // out_seq, out_seq, d), F32),
        scratch_shapes=[pltpu.VMEM((ts, d), F32), pltpu.SemaphoreType.DMA(())],
        compiler_params=_cp("arbitrary", "arbitrary"),
    )(hid, w_down, x1, mod3, norm_f_g.reshape(1, d))


def _pack_kernel(w_ref, o_ref, *, moves):
    o_ref[...] = jnp.zeros(o_ref.shape, BF16)
    for src, width, dst in moves:
        o_ref[:, dst:dst + width] = w_ref[:, src:src + width].astype(BF16)


def _pack_columns(w, layer, moves, out_cols):
    _, rows, cols = w.shape
    tr = _tile(rows, PACK_ROWS)
    return pl.pallas_call(
        functools.partial(_pack_kernel, moves=tuple(moves)),
        grid=(rows // tr,),
        in_specs=[pl.BlockSpec((None, tr, cols), lambda i: (layer, i, 0))],
        out_specs=pl.BlockSpec((tr, out_cols), lambda i: (i, 0)),
        out_shape=jax.ShapeDtypeStruct((rows, out_cols), BF16),
        compiler_params=_cp("parallel"),
    )(w)


def _rope_tables(n_tokens):
    t = jnp.arange(n_tokens, dtype=jnp.int32)
    row = (t // GRID_W).astype(F32)
    col = (t % GRID_W).astype(F32)
    n_freq = ROPE_DIM // 4
    inv_freq = ROPE_THETA ** (-jnp.arange(n_freq, dtype=F32) / n_freq)
    ang = jnp.concatenate([row[:, None] * inv_freq, col[:, None] * inv_freq], axis=-1)
    cos, sin = jnp.cos(ang), jnp.sin(ang)
    z = jnp.zeros_like(cos)
    return (jnp.concatenate([cos, cos, z, z], axis=-1),
            jnp.concatenate([-sin, z, z, z], axis=-1),
            jnp.concatenate([z, sin, z, z], axis=-1))


def _layer(x, mod3, row0, w, rope_tabs, attend, out_seq):
    bm, sm, d = x.shape
    m = bm * sm
    h, q_b, k_b, v_b = _qkv(x, w["norm1_g"], mod3, row0, w["w_q"], w["w_k"], w["w_v"], attend.cache_layout)
    h2d = h.reshape(m, d)
    q_a, ckv, kr128, k_a, v_a = _latent(h, w["w_lat"], w["q_norm_g"], w["kv_norm_g"], w["w_uq_p"],
                                        w["w_uk"], w["w_uv"], rope_tabs)
    o_a, o_b = attend(q_a, k_a, v_a, q_b, k_b, v_b)
    mm = _merge(o_a.reshape(m, -1), o_b.reshape(m, -1), h2d, w["w_oa"], w["w_ob"], w["w_ga"], w["w_gb"])
    x1, hn = _mixout(mm.reshape(bm, sm, d), x, w["w_out"], w["norm2_g"], mod3, row0)
    hid = _ffn_up(hn.reshape(m, d), w["w_gu"])
    x2 = _ffn_down(hid.reshape(bm, sm, -1), w["w_down"], x1, w["norm_f_g"], mod3, row0, out_seq)
    return x2, ckv, kr128, k_b, v_b


class _PromptAttend:
    cache_layout = True

    def __init__(self, batch, seq):
        self.batch, self.seq = batch, seq

    def __call__(self, q_a, k_a, v_a, q_b, k_b, v_b):
        sh = lambda a: a.reshape(self.batch, self.seq, -1)
        o_a = _attention(sh(q_a), sh(k_a), sh(v_a), None, None, heads=H_A, dv=V_DIM, hb=H_A, tq=self.seq)
        rows = lambda a: a.reshape(self.batch, self.seq * H_B, HD_B)
        o_b = _attention(sh(q_b), rows(k_b), rows(v_b), None, None, heads=H_B, dv=HD_B, hb=H_B,
                         tq=self.seq, head_rows=True)
        return o_a, o_b


class _SampleAttend:
    cache_layout = False

    def __init__(self, k_ctx_a, v_ctx_a, k_ctx_b, v_ctx_b, bias):
        self.ctx = (k_ctx_a, v_ctx_a, k_ctx_b, v_ctx_b, bias)

    def __call__(self, q_a, k_a, v_a, q_b, k_b, v_b):
        k_ctx_a, v_ctx_a, k_ctx_b, v_ctx_b, bias = self.ctx
        o_a = _attention(q_a, k_a, v_a, k_ctx_a, v_ctx_a, heads=H_A, dv=V_DIM, hb=1, tq=MLA_TQ, tk=MLA_TK)
        o_b = _na_attention(q_b, k_b, v_b, k_ctx_b, v_ctx_b, bias)
        return o_a, o_b


def kernel(x_prompt, x_sample, cache_mla_ckv, cache_mla_krope, cache_na_k, cache_na_v, c, c_ctx,
           w_mod, b_mod, norm1_g, w_in, q_norm_g, kv_norm_g, w_uq, w_uk, w_uv, rpb,
           w_oa, w_ob, w_out, norm2_g, w_gu, w_down, norm_f_g):
    batch, seq, d = x_prompt.shape
    dec_batch, dec_seq, _ = x_sample.shape
    depth = w_mod.shape[0]
    assert depth == 1, "one trunk layer: the final norm is fused into the layer's last kernel"
    past = cache_mla_ckv.shape[2]
    q_lora, kv_lora = q_norm_g.shape[-1], kv_norm_g.shape[-1]
    na_w = H_B * HD_B
    rows = dec_seq // GRID_W
    assert rows % NA_QROWS == 0 and rows >= NA_KROWS

    n_cond = 1 + dec_batch
    r8 = -(-n_cond // 8) * 8
    cond = jnp.zeros((r8, d), F32).at[0].set(c_ctx).at[1:n_cond].set(c)

    l = 0
    mod3 = _adaln(cond, w_mod[l], b_mod[l]).reshape(r8, 1, 6 * d)

    src, o = {}, 0
    for name, width in (("lat", q_lora + kv_lora + ROPE_DIM), ("q", na_w), ("k", na_w), ("v", na_w),
                        ("ga", d), ("gb", d)):
        src[name] = (o, width)
        o += width
    slot = dict({name: width for name, (_, width) in src.items()}, lat=q_lora + kv_lora + LANES)
    moves, views, o = [], {}, 0
    for name in sorted(slot, key=lambda s: -slot[s]):
        start = -(-o // slot[name]) * slot[name]
        moves.append(src[name] + (start,))
        views[name] = (slot[name], start // slot[name])
        o = start + slot[name]
    w_in_packed = _pack_columns(w_in, l, moves, o)
    view = lambda name: (w_in_packed,) + views[name]
    w_uq_p = jnp.pad(w_uq[l].reshape(q_lora, H_A, NOPE_DIM + ROPE_DIM),
                     ((0, 0), (0, 0), (0, QK_SLOT - NOPE_DIM - ROPE_DIM))).reshape(q_lora, H_A * QK_SLOT)
    w = {
        "w_lat": view("lat"), "w_q": view("q"), "w_k": view("k"), "w_v": view("v"),
        "w_ga": view("ga"), "w_gb": view("gb"),
        "w_uq_p": w_uq_p.astype(BF16), "w_uk": w_uk[l].astype(BF16), "w_uv": w_uv[l].astype(BF16),
        "w_oa": w_oa[l].astype(BF16), "w_ob": w_ob[l].astype(BF16), "w_out": w_out[l].astype(BF16),
        "w_gu": w_gu[l].astype(BF16), "w_down": w_down[l].astype(BF16),
        "norm1_g": norm1_g[l], "norm2_g": norm2_g[l], "q_norm_g": q_norm_g[l], "kv_norm_g": kv_norm_g[l],
        "norm_f_g": norm_f_g,
    }

    xp = x_prompt.reshape(1, batch * seq, d)
    yp, ckv_p, kr_p, k_b_p, v_b_p = _layer(xp, mod3, 0, w, None, _PromptAttend(batch, seq), seq)

    kr_ctx = jnp.pad(cache_mla_krope[:, l], ((0, 0), (0, 0), (0, LANES - ROPE_DIM)))
    k_ctx_a, v_ctx_a = _kvexp(cache_mla_ckv[:, l], kr_ctx, w["w_uk"], w["w_uv"])
    attend = _SampleAttend(k_ctx_a, v_ctx_a,
                           cache_na_k[:, l].reshape(dec_batch, past * H_B, HD_B),
                           cache_na_v[:, l].reshape(dec_batch, past * H_B, HD_B),
                           _na_bias_tables(rpb[l], rows))
    ys, _, _, _, _ = _layer(x_sample, mod3, 1, w, _rope_tables(dec_seq), attend, dec_seq)

    return (yp, ys,
            ckv_p.reshape(batch, 1, seq, kv_lora),
            kr_p.reshape(batch, seq, LANES)[:, :, :ROPE_DIM].reshape(batch, 1, seq, ROPE_DIM),
            k_b_p.reshape(batch, 1, seq, H_B, HD_B),
            v_b_p.reshape(batch, 1, seq, H_B, HD_B))
```

```python
import functools

import jax
import jax.numpy as jnp
import numpy as np
from jax import lax
from jax.experimental import pallas as pl
from jax.experimental.pallas import tpu as pltpu

F32 = jnp.float32
BF16 = jnp.bfloat16

GRID_W = 64
H_A = 8
NOPE_DIM = 128
ROPE_DIM = 64
V_DIM = 128
H_B = 8
HD_B = 128
WIN_R_MAX = 8
WIN_C = 16
ROPE_THETA = 10000.0
NORM_EPS = 1e-6
NEG_INF = -1e30
LOG2E = 1.4426950408889634
MLA_QSCALE = (NOPE_DIM + ROPE_DIM) ** -0.5 * LOG2E
NA_QSCALE = HD_B ** -0.5 * LOG2E

LANES = 128
MXU_COLS = 256
QK_SLOT = 256
NA_QROWS = 4
NA_KROWS = NA_QROWS + WIN_R_MAX
VMEM_LIMIT = 56 * 2 ** 20

ROW_TILE = 512
FFN_UP_ROWS = 1024
COL_TILE = 512
ADALN_COLS = 1024
MLA_TQ, MLA_TK = 1024, 512
ATTN_TK = 1024


def _cp(*sem):
    return pltpu.CompilerParams(dimension_semantics=sem, vmem_limit_bytes=VMEM_LIMIT)


def _dot(a, b):
    return jnp.dot(a, b, preferred_element_type=F32)


def _dot_nt(a, b):
    return lax.dot_general(a, b, (((1,), (1,)), ((), ())), preferred_element_type=F32)


def _sigmoid(x):
    return 1.0 / (1.0 + jnp.exp(-x))


def _rms(x, g):
    return x * lax.rsqrt(jnp.mean(x * x, axis=-1, keepdims=True) + NORM_EPS) * g


def _tile(n, want):
    t = min(n, want)
    while n % t:
        t //= 2
    return t


def _adaln_kernel(c_ref, w_ref, b_ref, o_ref):
    c = c_ref[...]
    s = (c * _sigmoid(c)).astype(BF16)
    o_ref[...] = _dot(s, w_ref[...].astype(BF16)) + b_ref[...]


def _adaln(cond, w_mod, b_mod):
    r, d = cond.shape
    n = w_mod.shape[1]
    tn = _tile(n, ADALN_COLS)
    return pl.pallas_call(
        _adaln_kernel,
        grid=(n // tn,),
        in_specs=[pl.BlockSpec((r, d), lambda j: (0, 0)),
                  pl.BlockSpec((d, tn), lambda j: (0, j)),
                  pl.BlockSpec((1, tn), lambda j: (0, j))],
        out_specs=pl.BlockSpec((r, tn), lambda j: (0, j)),
        out_shape=jax.ShapeDtypeStruct((r, n), F32),
        compiler_params=_cp("arbitrary"),
    )(cond, w_mod, b_mod.reshape(1, n))


def _mod_spec(d, sec, row0, grid_rank):
    if grid_rank == 2:
        return pl.BlockSpec((None, 1, d), lambda b, i: (row0 + b, 0, sec))
    return pl.BlockSpec((None, 1, d), lambda b, i, j: (row0 + b, 0, sec))


def _qkv_kernel(x_ref, g_ref, sh_ref, sc_ref, wq_ref, wk_ref, wv_ref,
                h_ref, q_ref, k_ref, v_ref, *, row_split, cache_layout):
    rs = x_ref.shape[0] // row_split
    for r in range(row_split):
        rows = slice(r * rs, (r + 1) * rs)
        h = (_rms(x_ref[rows, :], g_ref[...]) * (1.0 + sc_ref[...]) + sh_ref[...]).astype(BF16)
        h_ref[rows, :] = h
        q_ref[rows, :] = (_dot_nt(h, wq_ref[...]) * NA_QSCALE).astype(BF16)
        k = _dot_nt(h, wk_ref[...])
        v = _dot_nt(h, wv_ref[...])
        if cache_layout:
            for hh in range(H_B):
                head_rows = pl.ds(r * rs * H_B + hh, rs, stride=H_B)
                k_ref[head_rows, :] = k[:, hh * HD_B:(hh + 1) * HD_B]
                v_ref[head_rows, :] = v[:, hh * HD_B:(hh + 1) * HD_B]
        else:
            k_ref[rows, :] = k.astype(BF16)
            one = jnp.ones((rs, HD_B), BF16)
            for hh in range(H_B):
                v_ref[rows, 2 * hh * HD_B:(2 * hh + 1) * HD_B] = v[:, hh * HD_B:(hh + 1) * HD_B].astype(BF16)
                v_ref[rows, (2 * hh + 1) * HD_B:(2 * hh + 2) * HD_B] = one


def _qkv(x, g, mod3, row0, wq, wk, wv, cache_layout):
    bm, sm, d = x.shape
    n = wq.shape[0]
    ts = _tile(sm, ROW_TILE)
    wspec = pl.BlockSpec((n, d), lambda b, i: (0, 0), pipeline_mode=pl.Buffered(1))
    tok = lambda w: pl.BlockSpec((None, ts, w), lambda b, i: (b, i, 0))
    if cache_layout:
        kv_specs = [pl.BlockSpec((None, ts * H_B, HD_B), lambda b, i: (b, i, 0))] * 2
        kv_shapes = [jax.ShapeDtypeStruct((bm, sm * H_B, HD_B), F32)] * 2
    else:
        kv_specs = [tok(n), tok(2 * n)]
        kv_shapes = [jax.ShapeDtypeStruct((bm, sm, n), BF16), jax.ShapeDtypeStruct((bm, sm, 2 * n), BF16)]
    return pl.pallas_call(
        functools.partial(_qkv_kernel, row_split=2 if ts % 16 == 0 else 1, cache_layout=cache_layout),
        grid=(bm, sm // ts),
        in_specs=[tok(d), pl.BlockSpec((1, d), lambda b, i: (0, 0)),
                  _mod_spec(d, 0, row0, 2), _mod_spec(d, 1, row0, 2), wspec, wspec, wspec],
        out_specs=[tok(d), tok(n)] + kv_specs,
        out_shape=[jax.ShapeDtypeStruct((bm, sm, d), BF16), jax.ShapeDtypeStruct((bm, sm, n), BF16)] + kv_shapes,
        compiler_params=_cp("parallel", "parallel"),
    )(x, g.reshape(1, d), mod3, mod3, wq, wk, wv)


def _rope128(g, c, s1, s2):
    return g * c + pltpu.roll(g, 96, 1) * s1 + pltpu.roll(g, 32, 1) * s2


def _expand_kv(ckv, kr, wuk_ref, wuv_ref, k_ref, v_ref):
    c = ckv.astype(BF16)
    kn = _dot(c, wuk_ref[...])
    v = _dot(c, wuv_ref[...])
    krb = kr.astype(BF16)
    one = jnp.ones((c.shape[0], V_DIM), BF16)
    for hh in range(H_A):
        lo = hh * QK_SLOT
        k_ref[:, lo:lo + LANES] = kn[:, hh * NOPE_DIM:(hh + 1) * NOPE_DIM].astype(BF16)
        k_ref[:, lo + LANES:lo + QK_SLOT] = krb
        v_ref[:, 2 * hh * V_DIM:(2 * hh + 1) * V_DIM] = v[:, hh * V_DIM:(hh + 1) * V_DIM].astype(BF16)
        v_ref[:, (2 * hh + 1) * V_DIM:(2 * hh + 2) * V_DIM] = one


def _latent_kernel(*refs, q_lora, kv_lora, rope):
    if rope:
        (h_ref, wl_ref, qg_ref, kg_ref, wuq_ref, wuk_ref, wuv_ref, c_ref, s1_ref, s2_ref,
         q_ref, ckv_ref, kr_ref, k_ref, v_ref) = refs
    else:
        (h_ref, wl_ref, qg_ref, kg_ref, wuq_ref, wuk_ref, wuv_ref,
         q_ref, ckv_ref, kr_ref, k_ref, v_ref) = refs
    lat = _dot_nt(h_ref[...], wl_ref[...])
    cq = lat[:, :q_lora]
    ckv = _rms(lat[:, q_lora:q_lora + kv_lora], kg_ref[...])
    kr = lat[:, q_lora + kv_lora:]
    ckv_ref[...] = ckv
    if rope:
        c, s1, s2 = c_ref[...], s1_ref[...], s2_ref[...]
        kr = _rope128(kr, c, s1, s2)
    kr_ref[...] = kr
    _expand_kv(ckv, kr, wuk_ref, wuv_ref, k_ref, v_ref)
    qa = _dot(_rms(cq, qg_ref[...]).astype(BF16), wuq_ref[...])
    for hh in range(H_A):
        lo = hh * QK_SLOT
        q_ref[:, lo:lo + LANES] = (qa[:, lo:lo + LANES] * MLA_QSCALE).astype(BF16)
        g = qa[:, lo + LANES:lo + QK_SLOT]
        if rope:
            g = _rope128(g, c, s1, s2)
        q_ref[:, lo + LANES:lo + QK_SLOT] = (g * MLA_QSCALE).astype(BF16)


def _latent(h, w_lat, q_norm_g, kv_norm_g, w_uq_p, w_uk, w_uv, rope_tabs):
    bm, sm, d = h.shape
    q_lora, kv_lora = q_norm_g.shape[-1], kv_norm_g.shape[-1]
    ts = _tile(sm, ROW_TILE)
    rope = rope_tabs is not None
    tok = lambda w: pl.BlockSpec((None, ts, w), lambda b, i: (b, i, 0))
    full = lambda a: pl.BlockSpec(a.shape, lambda b, i: (0, 0))
    qg, kg = q_norm_g.reshape(1, q_lora), kv_norm_g.reshape(1, kv_lora)
    args = [h, w_lat, qg, kg, w_uq_p, w_uk, w_uv]
    in_specs = [tok(d)] + [full(a) for a in args[1:]]
    if rope:
        in_specs += [pl.BlockSpec((ts, LANES), lambda b, i: (i, 0))] * 3
        args += list(rope_tabs)
    widths = (H_A * QK_SLOT, kv_lora, LANES, H_A * QK_SLOT, H_A * 2 * V_DIM)
    dtypes = (BF16, F32, F32, BF16, BF16)
    return pl.pallas_call(
        functools.partial(_latent_kernel, q_lora=q_lora, kv_lora=kv_lora, rope=rope),
        grid=(bm, sm // ts),
        in_specs=in_specs,
        out_specs=[tok(w) for w in widths],
        out_shape=[jax.ShapeDtypeStruct((bm, sm, w), t) for w, t in zip(widths, dtypes)],
        compiler_params=_cp("parallel", "parallel"),
    )(*args)


def _kvexp_kernel(ckv_ref, kr_ref, wuk_ref, wuv_ref, k_ref, v_ref):
    _expand_kv(ckv_ref[...], kr_ref[...], wuk_ref, wuv_ref, k_ref, v_ref)


def _kvexp(ckv, kr128, w_uk, w_uv):
    bm, sm, kv_lora = ckv.shape
    ts = _tile(sm, ROW_TILE)
    tok = lambda w: pl.BlockSpec((None, ts, w), lambda b, i: (b, i, 0))
    full = lambda a: pl.BlockSpec(a.shape, lambda b, i: (0, 0))
    widths = (H_A * QK_SLOT, H_A * 2 * V_DIM)
    return pl.pallas_call(
        _kvexp_kernel,
        grid=(bm, sm // ts),
        in_specs=[tok(kv_lora), tok(LANES), full(w_uk), full(w_uv)],
        out_specs=[tok(w) for w in widths],
        out_shape=[jax.ShapeDtypeStruct((bm, sm, w), BF16) for w in widths],
        compiler_params=_cp("parallel", "parallel"),
    )(ckv, kr128, w_uk, w_uv)


def _attn_kernel(*refs, hb, dk, dv, vw, two, tk, head_rows):
    ones = vw == 2 * dv
    if two:
        q_ref, k1_ref, v1_ref, k2_ref, v2_ref, o_ref = refs
    else:
        q_ref, k1_ref, v1_ref, o_ref = refs
    n_keys = k1_ref.shape[0] // max(head_rows, 1)
    chunks = [(k1_ref, v1_ref, c * tk, tk) for c in range(n_keys // tk)]
    if two:
        chunks.append((k2_ref, v2_ref, 0, k2_ref.shape[0]))

    def head_slab(ref, lo, n, j, w):
        if head_rows:
            return ref[pl.ds(lo * head_rows + j, n, stride=head_rows), :]
        return ref[lo:lo + n, j * w:(j + 1) * w]

    for j in range(hb):
        q = q_ref[:, j * dk:(j + 1) * dk]

        def logits(ch):
            k_ref, _, lo, n = ch
            return _dot_nt(q, head_slab(k_ref, lo, n, j, dk).astype(BF16))

        s_next = logits(chunks[0])
        m = l = acc = None
        for i, (_, v_ref, lo, n) in enumerate(chunks):
            s = s_next
            if i + 1 < len(chunks):
                s_next = logits(chunks[i + 1])
            mc = jnp.max(s, axis=-1, keepdims=True)
            m_new = mc if m is None else jnp.maximum(m, mc)
            p = jnp.exp2(s - m_new)
            pv = _dot(p.astype(BF16), head_slab(v_ref, lo, n, j, vw).astype(BF16))
            if not ones:
                ps = jnp.sum(p, axis=-1, keepdims=True)
            if m is None:
                acc = pv
                l = None if ones else ps
            else:
                alpha = jnp.exp2(m - m_new)
                acc = alpha * acc + pv
                l = None if ones else alpha * l + ps
            m = m_new
        if ones:
            o = acc[:, :dv] * (1.0 / acc[:, dv:])
        else:
            o = acc * (1.0 / l)
        o_ref[:, j * dv:(j + 1) * dv] = o.astype(o_ref.dtype)


def _attention(q, k1, v1, k2, v2, *, heads, dv, hb, tq, tk=ATTN_TK, head_rows=False):
    b, sq, qw = q.shape
    dk = qw // heads
    two = k2 is not None
    tq = _tile(sq, tq)
    if head_rows:
        assert hb == heads and not two
        vw, n_keys = v1.shape[-1], k1.shape[1] // heads
        kv = lambda a, w: pl.BlockSpec((None, a.shape[1], w), lambda bi, hg, qi: (bi, 0, 0))
    else:
        vw, n_keys = v1.shape[-1] // heads, k1.shape[1]
        kv = lambda a, w: pl.BlockSpec((None, a.shape[1], hb * w), lambda bi, hg, qi: (bi, 0, hg))
    in_specs = [pl.BlockSpec((None, tq, hb * dk), lambda bi, hg, qi: (bi, qi, hg)),
                kv(k1, dk), kv(v1, vw)]
    args = [q, k1, v1]
    if two:
        in_specs += [kv(k2, dk), kv(v2, vw)]
        args += [k2, v2]
    return pl.pallas_call(
        functools.partial(_attn_kernel, hb=hb, dk=dk, dv=dv, vw=vw, two=two,
                          tk=_tile(n_keys, tk), head_rows=heads if head_rows else 0),
        grid=(b, heads // hb, sq // tq),
        in_specs=in_specs,
        out_specs=pl.BlockSpec((None, tq, hb * dv), lambda bi, hg, qi: (bi, qi, hg)),
        out_shape=jax.ShapeDtypeStruct((b, sq, heads * dv), BF16),
        compiler_params=_cp("parallel", "parallel", "arbitrary"),
    )(*args)


def _na_bias_tables(rpb, rows):
    win_r = min(WIN_R_MAX, rows)
    qc = np.arange(GRID_W)[:, None]
    kc = np.arange(GRID_W)[None, :]
    qs = np.clip(qc - WIN_C // 2, 0, GRID_W - WIN_C)
    col_valid = (kc >= qs) & (kc < qs + WIN_C)
    col_off = np.clip(kc - qc + (WIN_C - 1), 0, 2 * WIN_C - 2)
    onehot = (col_off[None] == np.arange(2 * WIN_C - 1)[:, None, None]).astype(np.float32)
    e = jnp.einsum("hrc,cqk->hrqk", rpb, onehot, precision=lax.Precision.HIGHEST)
    e = jnp.where(col_valid, e * LOG2E, NEG_INF)
    n_roff = 2 * WIN_R_MAX - 1
    e = jnp.concatenate([e, jnp.full((H_B, 1, GRID_W, GRID_W), NEG_INF, F32)], axis=1)
    slab = []
    for r_blk in (0, NA_QROWS, rows - NA_QROWS):
        kr0 = int(np.clip(r_blk - win_r // 2, 0, rows - NA_KROWS))
        per_row = []
        for a in range(NA_QROWS):
            r = r_blk + a
            r0 = int(np.clip(r - win_r // 2, 0, rows - win_r))
            per_row.append([kr - r + (WIN_R_MAX - 1) if r0 <= kr < r0 + win_r else n_roff
                            for kr in range(kr0, kr0 + NA_KROWS)])
        slab.append(per_row)

    def assemble(e_ref, o_ref):
        for v, per_row in enumerate(slab):
            for a, idx in enumerate(per_row):
                o_ref[v, a * GRID_W:(a + 1) * GRID_W, :] = jnp.concatenate([e_ref[j] for j in idx], axis=-1)

    return pl.pallas_call(
        assemble,
        grid=(H_B,),
        in_specs=[pl.BlockSpec((None, n_roff + 1, GRID_W, GRID_W), lambda h: (h, 0, 0, 0))],
        out_specs=pl.BlockSpec((None, 3, NA_QROWS * GRID_W, NA_KROWS * GRID_W), lambda h: (h, 0, 0, 0)),
        out_shape=jax.ShapeDtypeStruct((H_B, 3, NA_QROWS * GRID_W, NA_KROWS * GRID_W), F32),
        compiler_params=_cp("parallel"),
    )(e)


def _na_kernel(q_ref, k_ref, v_ref, kc_ref, vc_ref, bias_ref, o_ref, *, rows):
    nblk = rows // NA_QROWS
    nq, nk = NA_QROWS * GRID_W, NA_KROWS * GRID_W
    n_ctx = kc_ref.shape[0] // H_B
    head_rows = pl.ds(pl.program_id(1), n_ctx, stride=H_B)
    kc = kc_ref[head_rows, :].astype(BF16)
    vc = jnp.concatenate([vc_ref[head_rows, :].astype(BF16), jnp.ones((n_ctx, HD_B), BF16)], axis=-1)

    for t in range(nblk):
        r_blk = t * NA_QROWS
        kr0 = int(np.clip(r_blk - min(WIN_R_MAX, rows) // 2, 0, rows - NA_KROWS))
        var = 0 if t == 0 else (2 if t == nblk - 1 else 1)
        q0, k0 = r_blk * GRID_W, kr0 * GRID_W
        q = q_ref[q0:q0 + nq, :]
        s = _dot_nt(q, k_ref[k0:k0 + nk, :]) + bias_ref[var]
        sc = _dot_nt(q, kc)
        m = jnp.maximum(jnp.max(s, axis=-1, keepdims=True), jnp.max(sc, axis=-1, keepdims=True))
        p = jnp.exp2(s - m)
        pc = jnp.exp2(sc - m)
        o = _dot(p.astype(BF16), v_ref[k0:k0 + nk, :]) + _dot(pc.astype(BF16), vc)
        o_ref[q0:q0 + nq, :] = (o[:, :HD_B] * (1.0 / o[:, HD_B:])).astype(o_ref.dtype)


def _na_attention(q, k, v, k_ctx, v_ctx, bias):
    b, s, _ = q.shape
    rows = s // GRID_W
    head = lambda n: pl.BlockSpec((None, n, HD_B), lambda bi, h: (bi, 0, h))
    ctx = pl.BlockSpec((None,) + k_ctx.shape[1:], lambda bi, h: (bi, 0, 0))
    return pl.pallas_call(
        functools.partial(_na_kernel, rows=rows),
        grid=(b, H_B),
        in_specs=[head(s), head(s),
                  pl.BlockSpec((None, s, 2 * HD_B), lambda bi, h: (bi, 0, h)),
                  ctx, ctx,
                  pl.BlockSpec((None,) + bias.shape[1:], lambda bi, h: (h, 0, 0, 0))],
        out_specs=head(s),
        out_shape=jax.ShapeDtypeStruct((b, s, H_B * HD_B), BF16),
        compiler_params=_cp("parallel", "arbitrary"),
    )(q, k, v, k_ctx, v_ctx, bias)


def _merge_kernel(oa_ref, ob_ref, h_ref, woa_ref, wob_ref, wga_ref, wgb_ref, m_ref, *, row_split):
    rs = h_ref.shape[0] // row_split
    for r in range(row_split):
        rows = slice(r * rs, (r + 1) * rs)
        h = h_ref[rows, :]
        ya = _dot(oa_ref[rows, :], woa_ref[...])
        yb = _dot(ob_ref[rows, :], wob_ref[...])
        ga = _sigmoid(_dot_nt(h, wga_ref[...]))
        gb = _sigmoid(_dot_nt(h, wgb_ref[...]))
        m_ref[rows, :] = (ga * ya + gb * yb).astype(BF16)


def _merge(oa, ob, h, w_oa, w_ob, w_ga, w_gb):
    m, d = h.shape
    n = w_oa.shape[1]
    tm = _tile(m, ROW_TILE)
    row = lambda a: pl.BlockSpec((tm, a.shape[1]), lambda i: (i, 0))
    full = lambda a: pl.BlockSpec(a.shape, lambda i: (0, 0), pipeline_mode=pl.Buffered(1))
    return pl.pallas_call(
        functools.partial(_merge_kernel, row_split=2 if tm % 16 == 0 else 1),
        grid=(m // tm,),
        in_specs=[row(oa), row(ob), row(h), full(w_oa), full(w_ob), full(w_ga), full(w_gb)],
        out_specs=pl.BlockSpec((tm, n), lambda i: (i, 0)),
        out_shape=jax.ShapeDtypeStruct((m, n), BF16),
        compiler_params=_cp("parallel"),
    )(oa, ob, h, w_oa, w_ob, w_ga, w_gb)


def _mixout_kernel(m_ref, x_ref, w_ref, gt_ref, g_ref, sh_ref, sc_ref, x1_ref, h2_ref, *, row_split):
    rs = m_ref.shape[0] // row_split
    for r in range(row_split):
        rows = slice(r * rs, (r + 1) * rs)
        x1 = x_ref[rows, :] + gt_ref[...] * _dot(m_ref[rows, :], w_ref[...])
        x1_ref[rows, :] = x1
        h2_ref[rows, :] = (_rms(x1, g_ref[...]) * (1.0 + sc_ref[...]) + sh_ref[...]).astype(BF16)


def _mixout(mm, x, w_out, g2, mod3, row0):
    bm, sm, d = x.shape
    ts = _tile(sm, ROW_TILE)
    tok = pl.BlockSpec((None, ts, d), lambda b, i: (b, i, 0))
    return pl.pallas_call(
        functools.partial(_mixout_kernel, row_split=2 if ts % 16 == 0 else 1),
        grid=(bm, sm // ts),
        in_specs=[tok, tok, pl.BlockSpec((d, d), lambda b, i: (0, 0), pipeline_mode=pl.Buffered(1)),
                  _mod_spec(d, 2, row0, 2), pl.BlockSpec((1, d), lambda b, i: (0, 0)),
                  _mod_spec(d, 3, row0, 2), _mod_spec(d, 4, row0, 2)],
        out_specs=[tok, tok],
        out_shape=[jax.ShapeDtypeStruct((bm, sm, d), F32),
                   jax.ShapeDtypeStruct((bm, sm, d), BF16)],
        compiler_params=_cp("parallel", "parallel"),
    )(mm, x, w_out, mod3, g2.reshape(1, d), mod3, mod3)


def _ffn_up_kernel(h_ref, wg_ref, wu_ref, o_ref, *, row_split):
    rs = h_ref.shape[0] // row_split
    for r in range(row_split):
        h = h_ref[r * rs:(r + 1) * rs, :]
        g = _dot(h, wg_ref[...])
        u = _dot(h, wu_ref[...])
        o_ref[r * rs:(r + 1) * rs, :] = (g * _sigmoid(g) * u).astype(BF16)


def _ffn_up(h2, w_gu):
    m, d = h2.shape
    d_ff = w_gu.shape[1] // 2
    tm = _tile(m, FFN_UP_ROWS)
    tn = d_ff // 2 if d_ff % (2 * MXU_COLS) == 0 else _tile(d_ff, COL_TILE)
    nj = d_ff // tn
    wspec = lambda off: pl.BlockSpec((d, tn), lambda j, i: (0, j + off), pipeline_mode=pl.Buffered(1))
    return pl.pallas_call(
        functools.partial(_ffn_up_kernel, row_split=4 if tm % 32 == 0 else 1),
        grid=(nj, m // tm),
        in_specs=[pl.BlockSpec((tm, d), lambda j, i: (i, 0)), wspec(0), wspec(nj)],
        out_specs=pl.BlockSpec((tm, tn), lambda j, i: (i, j)),
        out_shape=jax.ShapeDtypeStruct((m, d_ff), BF16),
        compiler_params=_cp("arbitrary", "arbitrary"),
    )(h2, w_gu, w_gu)


def _ffn_down_kernel(a_ref, w_ref, x_hbm, gt_ref, g_ref, y_ref, x_buf, x_sem, *, row_split):
    b, i = pl.program_id(0), pl.program_id(1)
    ts, d = x_buf.shape
    x_copy = pltpu.make_async_copy(x_hbm.at[b, pl.ds(i * ts, ts), :], x_buf, x_sem)
    x_copy.start()
    rs = ts // row_split
    tn = _tile(d, COL_TILE)
    for r in range(row_split):
        rows = slice(r * rs, (r + 1) * rs)
        yr = y_ref.at[r] if len(y_ref.shape) == 3 else y_ref.at[pl.ds(r * rs, rs)]
        for n in range(d // tn):
            cols = slice(n * tn, (n + 1) * tn)
            yr[:, cols] = _dot(a_ref[rows, :], w_ref[:, cols])
        if r == 0:
            x_copy.wait()
        yr[...] = _rms(x_buf[rows, :] + gt_ref[...] * yr[...], g_ref[...])


def _ffn_down(hid, w_down, x1, norm_f_g, mod3, row0, out_seq):
    bm, sm, d = x1.shape
    d_ff = w_down.shape[0]
    ts = _tile(sm, ROW_TILE)
    if out_seq >= ts:
        assert out_seq == sm
        row_split = 2 if ts % 16 == 0 else 1
        out_spec = pl.BlockSpec((None, ts, d), lambda b, i: (b, i, 0))
    else:
        assert bm == 1 and ts % out_seq == 0
        row_split = ts // out_seq
        out_spec = pl.BlockSpec((row_split, out_seq, d), lambda b, i: (i, 0, 0))
    return pl.pallas_call(
        functools.partial(_ffn_down_kernel, row_split=row_split),
        grid=(bm, sm // ts),
        in_specs=[pl.BlockSpec((None, ts, d_ff), lambda b, i: (b, i, 0)),
                  pl.BlockSpec((d_ff, d), lambda b, i: (0, 0), pipeline_mode=pl.Buffered(1)),
                  pl.BlockSpec(memory_space=pl.ANY),
                  _mod_spec(d, 5, row0, 2),
                  pl.BlockSpec((1, d), lambda b, i: (0, 0))],
        out_specs=out_spec,
        out_shape=jax.ShapeDtypeStruct((bm * sm // out_seq, out_seq, d), F32),
        scratch_shapes=[pltpu.VMEM((ts, d), F32), pltpu.SemaphoreType.DMA(())],
        compiler_params=_cp("arbitrary", "arbitrary"),
    )(hid, w_down, x1, mod3, norm_f_g.reshape(1, d))


def _rope_tables(n_tokens):
    t = jnp.arange(n_tokens, dtype=jnp.int32)
    row = (t // GRID_W).astype(F32)
    col = (t % GRID_W).astype(F32)
    n_freq = ROPE_DIM // 4
    inv_freq = ROPE_THETA ** (-jnp.arange(n_freq, dtype=F32) / n_freq)
    ang = jnp.concatenate([row[:, None] * inv_freq, col[:, None] * inv_freq], axis=-1)
    cos, sin = jnp.cos(ang), jnp.sin(ang)
    z = jnp.zeros_like(cos)
    return (jnp.concatenate([cos, cos, z, z], axis=-1),
            jnp.concatenate([-sin, z, z, z], axis=-1),
            jnp.concatenate([z, sin, z, z], axis=-1))


def _layer(x, mod3, row0, w, rope_tabs, attend, out_seq):
    bm, sm, d = x.shape
    m = bm * sm
    h, q_b, k_b, v_b = _qkv(x, w["norm1_g"], mod3, row0, w["w_q"], w["w_k"], w["w_v"], attend.cache_layout)
    h2d = h.reshape(m, d)
    q_a, ckv, kr128, k_a, v_a = _latent(h, w["w_lat"], w["q_norm_g"], w["kv_norm_g"], w["w_uq_p"],
                                        w["w_uk"], w["w_uv"], rope_tabs)
    o_a, o_b = attend(q_a, k_a, v_a, q_b, k_b, v_b)
    mm = _merge(o_a.reshape(m, -1), o_b.reshape(m, -1), h2d, w["w_oa"], w["w_ob"], w["w_ga"], w["w_gb"])
    x1, hn = _mixout(mm.reshape(bm, sm, d), x, w["w_out"], w["norm2_g"], mod3, row0)
    hid = _ffn_up(hn.reshape(m, d), w["w_gu"])
    x2 = _ffn_down(hid.reshape(bm, sm, -1), w["w_down"], x1, w["norm_f_g"], mod3, row0, out_seq)
    return x2, ckv, kr128, k_b, v_b


class _PromptAttend:
    cache_layout = True

    def __init__(self, batch, seq):
        self.batch, self.seq = batch, seq

    def __call__(self, q_a, k_a, v_a, q_b, k_b, v_b):
        sh = lambda a: a.reshape(self.batch, self.seq, -1)
        o_a = _attention(sh(q_a), sh(k_a), sh(v_a), None, None, heads=H_A, dv=V_DIM, hb=H_A, tq=self.seq)
        rows = lambda a: a.reshape(self.batch, self.seq * H_B, HD_B)
        o_b = _attention(sh(q_b), rows(k_b), rows(v_b), None, None, heads=H_B, dv=HD_B, hb=H_B,
                         tq=self.seq, head_rows=True)
        return o_a, o_b


class _SampleAttend:
    cache_layout = False

    def __init__(self, k_ctx_a, v_ctx_a, k_ctx_b, v_ctx_b, bias):
        self.ctx = (k_ctx_a, v_ctx_a, k_ctx_b, v_ctx_b, bias)

    def __call__(self, q_a, k_a, v_a, q_b, k_b, v_b):
        k_ctx_a, v_ctx_a, k_ctx_b, v_ctx_b, bias = self.ctx
        o_a = _attention(q_a, k_a, v_a, k_ctx_a, v_ctx_a, heads=H_A, dv=V_DIM, hb=1, tq=MLA_TQ, tk=MLA_TK)
        o_b = _na_attention(q_b, k_b, v_b, k_ctx_b, v_ctx_b, bias)
        return o_a, o_b


def kernel(x_prompt, x_sample, cache_mla_ckv, cache_mla_krope, cache_na_k, cache_na_v, c, c_ctx,
           w_mod, b_mod, norm1_g, w_in, q_norm_g, kv_norm_g, w_uq, w_uk, w_uv, rpb,
           w_oa, w_ob, w_out, norm2_g, w_gu, w_down, norm_f_g):
    batch, seq, d = x_prompt.shape
    dec_batch, dec_seq, _ = x_sample.shape
    depth = w_mod.shape[0]
    assert depth == 1, "one trunk layer: the final norm is fused into the layer's last kernel"
    past = cache_mla_ckv.shape[2]
    q_lora, kv_lora = q_norm_g.shape[-1], kv_norm_g.shape[-1]
    na_w = H_B * HD_B
    rows = dec_seq // GRID_W
    assert rows % NA_QROWS == 0 and rows >= NA_KROWS

    n_cond = 1 + dec_batch
    r8 = -(-n_cond // 8) * 8
    cond = jnp.zeros((r8, d), F32).at[0].set(c_ctx).at[1:n_cond].set(c)

    l = 0
    mod3 = _adaln(cond, w_mod[l], b_mod[l]).reshape(r8, 1, 6 * d)

    wi = jnp.swapaxes(w_in, 1, 2)[l]
    o = 0
    sec = {}
    for name, width in (("lat", q_lora + kv_lora + ROPE_DIM), ("q", na_w), ("k", na_w), ("v", na_w),
                        ("ga", d), ("gb", d)):
        sec[name] = wi[o:o + width, :]
        o += width
    lat_w = q_lora + kv_lora + LANES
    w_uq_p = jnp.pad(w_uq[l].reshape(q_lora, H_A, NOPE_DIM + ROPE_DIM),
                     ((0, 0), (0, 0), (0, QK_SLOT - NOPE_DIM - ROPE_DIM))).reshape(q_lora, H_A * QK_SLOT)
    w = {
        "w_lat": jnp.pad(sec["lat"], ((0, lat_w - sec["lat"].shape[0]), (0, 0))).astype(BF16),
        "w_q": sec["q"].astype(BF16), "w_k": sec["k"].astype(BF16), "w_v": sec["v"].astype(BF16),
        "w_ga": sec["ga"].astype(BF16), "w_gb": sec["gb"].astype(BF16),
        "w_uq_p": w_uq_p.astype(BF16), "w_uk": w_uk[l].astype(BF16), "w_uv": w_uv[l].astype(BF16),
        "w_oa": w_oa[l].astype(BF16), "w_ob": w_ob[l].astype(BF16), "w_out": w_out[l].astype(BF16),
        "w_gu": w_gu[l].astype(BF16), "w_down": w_down[l].astype(BF16),
        "norm1_g": norm1_g[l], "norm2_g": norm2_g[l], "q_norm_g": q_norm_g[l], "kv_norm_g": kv_norm_g[l],
        "norm_f_g": norm_f_g,
    }

    xp = x_prompt.reshape(1, batch * seq, d)
    yp, ckv_p, kr_p, k_b_p, v_b_p = _layer(xp, mod3, 0, w, None, _PromptAttend(batch, seq), seq)

    kr_ctx = jnp.pad(cache_mla_krope[:, l], ((0, 0), (0, 0), (0, LANES - ROPE_DIM)))
    k_ctx_a, v_ctx_a = _kvexp(cache_mla_ckv[:, l], kr_ctx, w["w_uk"], w["w_uv"])
    attend = _SampleAttend(k_ctx_a, v_ctx_a,
                           cache_na_k[:, l].reshape(dec_batch, past * H_B, HD_B),
                           cache_na_v[:, l].reshape(dec_batch, past * H_B, HD_B),
                           _na_bias_tables(rpb[l], rows))
    ys, _, _, _, _ = _layer(x_sample, mod3, 1, w, _rope_tables(dec_seq), attend, dec_seq)

    return (yp, ys,
            ckv_p.reshape(batch, 1, seq, kv_lora),
            kr_p.reshape(batch, seq, LANES)[:, :, :ROPE_DIM].reshape(batch, 1, seq, ROPE_DIM),
            k_b_p.reshape(batch, 1, seq, H_B, HD_B),
            v_b_p.reshape(batch, 1, seq, H_B, HD_B))
```
